```python
import functools
import jax, jax.numpy as jnp
from jax import lax
import numpy as np

D_MODEL = 2048
BATCH = 16
SEQ = 256
DEPTH = 4
DEC_BATCH = 8
DEC_SEQ = 2048
PAST_LEN = 512

GRID_W = 64
HEAD_DIM = 64
MIX_W = D_MODEL
A_W = 3 * MIX_W // 8
B_W = MIX_W // 4
C_W = MIX_W - A_W - B_W
HA = A_W // HEAD_DIM
HC = C_W // HEAD_DIM
WIN_R = 8
WIN_C = 16
QBLK_C = WIN_C
KBLK_C = 2 * WIN_C
CONV_K = 31
W_RANK = 64
A_RANK = 64
G_RANK = 128
D_FF = 11 * D_MODEL // 4
N_MOD = 6
Q_BLOCK = 128
SPLIT_SIZES = (A_W, A_W, A_W, B_W, B_W, C_W, C_W, C_W, W_RANK, W_RANK, A_RANK, A_RANK, G_RANK)
IN_COLS = 3 * A_W + 2 * B_W + 3 * C_W + 2 * W_RANK + 2 * A_RANK + G_RANK
ATTN_SCALE = HEAD_DIM ** -0.5
DECAY_SCALE = 0.606531
RMS_EPS = 1e-6
LN_EPS = 1e-5
GN_EPS = 64e-5
L2_EPS = 1e-12
MASK_VALUE = -1e30

kernel_name = 'hybrid_diffusion_natten_conformer_rwkv7_step'


def rmsnorm(x, g):
    x32 = x.astype(jnp.float32)
    y = x32 * lax.rsqrt(jnp.mean(x32 * x32, axis=-1, keepdims=True) + RMS_EPS)
    return (y * g.astype(jnp.float32)).astype(x.dtype)


def ada_modulation(cvec, w_mod, b_mod):
    m = jax.nn.silu(cvec) @ w_mod + b_mod
    return jnp.split(m[:, None, :], N_MOD, axis=-1)


def split_columns(proj):
    cuts, acc = [], 0
    for s in SPLIT_SIZES[:-1]:
        acc += s
        cuts.append(acc)
    return jnp.split(proj, cuts, axis=-1)


def depthwise_conv(x, w, b):
    k = w.shape[0]
    y = lax.conv_general_dilated(x, w[:, None, :].astype(x.dtype), window_strides=(1,),
                                 padding=[((k - 1) // 2, k // 2)],
                                 dimension_numbers=('NWC', 'WIO', 'NWC'),
                                 feature_group_count=x.shape[-1])
    return y + b.astype(x.dtype)


def context_attention(q, k, v):
    bsz, s_len, n_h, d_h = q.shape
    nb = s_len // Q_BLOCK
    qb = jnp.moveaxis(q.reshape(bsz, nb, Q_BLOCK, n_h, d_h), 1, 0)

    def block(qi):
        s = jnp.einsum('bqhd,bkhd->bhqk', qi, k).astype(jnp.float32) * ATTN_SCALE
        p = jax.nn.softmax(s, axis=-1).astype(v.dtype)
        return jnp.einsum('bhqk,bkhd->bqhd', p, v)

    out = lax.map(block, qb)
    return jnp.moveaxis(out, 0, 1).reshape(bsz, s_len, n_h * d_h)


def neighbourhood_attention(q, k, v, ctx_k, ctx_v, rpb):
    bsz, t_len, n_h, d_h = q.shape
    rows = t_len // GRID_W
    kr = min(WIN_R, rows)
    n_cb = GRID_W // QBLK_C
    qg = q.reshape(bsz, rows, GRID_W, n_h, d_h)
    kg = k.reshape(bsz, rows, GRID_W, n_h, d_h)
    vg = v.reshape(bsz, rows, GRID_W, n_h, d_h)
    qcol = np.arange(GRID_W).reshape(n_cb, QBLK_C)
    key_start = np.clip(np.arange(n_cb) * QBLK_C - WIN_C // 2, 0, GRID_W - KBLK_C)
    col_idx = key_start[:, None] + np.arange(KBLK_C)[None, :]
    win_start = np.clip(qcol - WIN_C // 2, 0, GRID_W - WIN_C)
    kcol = col_idx[:, None, :]
    col_valid = (kcol >= win_start[:, :, None]) & (kcol < win_start[:, :, None] + WIN_C)
    col_bias_idx = np.clip(kcol - qcol[:, :, None] + WIN_C - 1, 0, 2 * WIN_C - 2)
    rpb32 = rpb.astype(jnp.float32)

    def row_block(i):
        rs = jnp.clip(i - kr // 2, 0, rows - kr)
        k_blk = lax.dynamic_slice_in_dim(kg, rs, kr, axis=1)[:, :, col_idx]
        v_blk = lax.dynamic_slice_in_dim(vg, rs, kr, axis=1)[:, :, col_idx]
        q_row = lax.dynamic_index_in_dim(qg, i, axis=1, keepdims=False).reshape(bsz, n_cb, QBLK_C, n_h, d_h)
        row_bias_idx = rs + jnp.arange(kr) - i + WIN_R - 1
        bias = rpb32[:, row_bias_idx][:, :, col_bias_idx].transpose(0, 2, 3, 1, 4)
        s_loc = jnp.einsum('bnqhd,brnkhd->bhnqrk', q_row, k_blk).astype(jnp.float32) * ATTN_SCALE + bias[None]
        s_loc = jnp.where(col_valid[:, :, None, :], s_loc, MASK_VALUE)
        s_loc = s_loc.reshape(bsz, n_h, n_cb, QBLK_C, kr * KBLK_C)
        s_ctx = jnp.einsum('bnqhd,bphd->bhnqp', q_row, ctx_k).astype(jnp.float32) * ATTN_SCALE
        p = jax.nn.softmax(jnp.concatenate([s_loc, s_ctx], axis=-1), axis=-1).astype(v.dtype)
        p_loc = p[..., :kr * KBLK_C].reshape(bsz, n_h, n_cb, QBLK_C, kr, KBLK_C)
        p_ctx = p[..., kr * KBLK_C:]
        o = (jnp.einsum('bhnqrk,brnkhd->bnqhd', p_loc, v_blk)
             + jnp.einsum('bhnqp,bphd->bnqhd', p_ctx, ctx_v.astype(v.dtype)))
        return o.reshape(bsz, GRID_W, n_h, d_h)

    out = lax.map(row_block, jnp.arange(rows))
    return jnp.moveaxis(out, 0, 1).reshape(bsz, t_len, n_h * d_h)


def conv_module(glu_a, glu_b, conv_w, conv_b, ln_g, ln_b):
    u = glu_a * jax.nn.sigmoid(glu_b)
    u = depthwise_conv(u, conv_w, conv_b).astype(jnp.float32)
    mu = jnp.mean(u, axis=-1, keepdims=True)
    var = jnp.mean(jnp.square(u - mu), axis=-1, keepdims=True)
    u = (u - mu) * lax.rsqrt(var + LN_EPS) * ln_g.astype(jnp.float32) + ln_b.astype(jnp.float32)
    return jax.nn.silu(u).astype(glu_a.dtype)


def wkv_scan(r, w, k, v, z, b, s0, reverse):
    xs = tuple(jnp.moveaxis(t, 1, 0) for t in (r, w, k, v, z, b))

    def step(s, inp):
        r_t, w_t, k_t, v_t, z_t, b_t = inp
        sz = jnp.einsum('bhvk,bhk->bhv', s, z_t)
        s = s * w_t[:, :, None, :] + sz[..., None] * b_t[:, :, None, :] + v_t[..., None] * k_t[:, :, None, :]
        return s, jnp.einsum('bhvk,bhk->bhv', s, r_t)

    s_final, o = lax.scan(step, s0, xs, reverse=reverse)
    return jnp.moveaxis(o, 0, 1), s_final


def rwkv7_bidirectional(r, k, v, xw_f, xw_b, xa_f, xa_b, xg, lp, s0_f, s0_b):
    f32 = jnp.float32
    bsz, t_len, _ = r.shape
    hd = lambda t: t.astype(f32).reshape(bsz, t_len, HC, HEAD_DIM)
    r_h, k_h, v_h = hd(r), hd(k), hd(v)
    kk = k_h * lp['k_k'].astype(f32).reshape(HC, HEAD_DIM)
    kk = kk * lax.rsqrt(jnp.sum(kk * kk, axis=-1, keepdims=True) + L2_EPS)
    k_a = lp['k_a'].astype(f32).reshape(HC, HEAD_DIM)
    r_k = lp['r_k'].astype(f32)

    def one_direction(xw, xa, d, s0, reverse):
        w = jnp.exp(-DECAY_SCALE * jax.nn.sigmoid(
            lp['w0'][d].astype(f32) + jnp.tanh(xw.astype(f32)) @ lp['w_lora'][d].astype(f32)))
        a = jax.nn.sigmoid(lp['a0'][d].astype(f32) + xa.astype(f32) @ lp['a_lora'][d].astype(f32))
        w, a = hd(w), hd(a)
        k_d = k_h * (1.0 + (a - 1.0) * k_a)
        o, s_fin = wkv_scan(r_h, w, k_d, v_h, -kk, kk * a, s0.astype(f32), reverse)
        bonus = jnp.sum(r_h * k_d * r_k, axis=-1, keepdims=True) * v_h
        return o, bonus, s_fin

    o_f, bonus_f, s_f = one_direction(xw_f, xa_f, 0, s0_f, False)
    o_b, bonus_b, s_b = one_direction(xw_b, xa_b, 1, s0_b, True)
    o = o_f + o_b
    mu = jnp.mean(o, axis=-1, keepdims=True)
    var = jnp.mean(jnp.square(o - mu), axis=-1, keepdims=True)
    o = ((o - mu) * lax.rsqrt(var + GN_EPS)).reshape(bsz, t_len, C_W)
    o = o * lp['ln_g'].astype(f32) + lp['ln_b'].astype(f32) + (bonus_f + bonus_b).reshape(bsz, t_len, C_W)
    g = jax.nn.sigmoid(xg.astype(f32)) @ lp['g_lora'].astype(f32)
    return o * g, s_f, s_b


def conv_ffn(h, w_up, dw_w, dw_b, w_down):
    u, gv = jnp.split(h @ w_up, 2, axis=-1)
    u = depthwise_conv(u, dw_w, dw_b)
    return (jax.nn.silu(u) * gv) @ w_down


def trunk_layer(x, cvec, attend, s0_f, s0_b, lp):
    bsz, t_len, _ = x.shape
    sh1, sc1, g1, sh2, sc2, g2 = ada_modulation(cvec, lp['w_mod'], lp['b_mod'])
    h = rmsnorm(x, lp['n1']) * (1.0 + sc1) + sh1
    (q, k, v, glu_a, glu_b, r, kc, vc, xw_f, xw_b, xa_f, xa_b, xg) = split_columns(h @ lp['w_in'])
    to_heads = lambda t: t.reshape(bsz, t_len, HA, HEAD_DIM)
    k_h, v_h = to_heads(k), to_heads(v)
    y_a = attend(to_heads(q), k_h, v_h)
    y_b = conv_module(glu_a, glu_b, lp['conv_w'], lp['conv_b'], lp['conv_ln_g'], lp['conv_ln_b'])
    y_c, s_f, s_b = rwkv7_bidirectional(r, kc, vc, xw_f, xw_b, xa_f, xa_b, xg, lp, s0_f, s0_b)
    y = jnp.concatenate([y_a.astype(x.dtype), y_b, y_c.astype(x.dtype)], axis=-1) @ lp['w_out']
    x = x + g1 * y
    h2 = rmsnorm(x, lp['n2']) * (1.0 + sc2) + sh2
    x = x + g2 * conv_ffn(h2, lp['w_up'], lp['ffn_conv_w'], lp['ffn_conv_b'], lp['w_down'])
    return x, k_h, v_h, s_f.astype(x.dtype), s_b.astype(x.dtype)


def setup_inputs(seed: int = 0) -> dict:
    key = jax.random.key(seed)
    keys = jax.random.split(key, 40)

    def nrm(i, shape, scale):
        return jax.random.normal(keys[i], shape, jnp.float32) * scale

    L = DEPTH
    return {
        'x_prompt': nrm(0, (BATCH, SEQ, D_MODEL), 1.0),
        'x_sample': nrm(1, (DEC_BATCH, DEC_SEQ, D_MODEL), 1.0),
        'cache_k': nrm(2, (DEC_BATCH, L, PAST_LEN, HA, HEAD_DIM), 1.0),
        'cache_v': nrm(3, (DEC_BATCH, L, PAST_LEN, HA, HEAD_DIM), 1.0),
        'state_fwd': nrm(4, (DEC_BATCH, L, HC, HEAD_DIM, HEAD_DIM), 0.3),
        'state_bwd': nrm(5, (DEC_BATCH, L, HC, HEAD_DIM, HEAD_DIM), 0.3),
        'c': nrm(6, (DEC_BATCH, D_MODEL), 1.0),
        'c_ctx': nrm(7, (D_MODEL,), 1.0),
        'norm1_g': 1.0 + nrm(8, (L, D_MODEL), 0.1),
        'norm2_g': 1.0 + nrm(9, (L, D_MODEL), 0.1),
        'w_mod': nrm(10, (L, D_MODEL, N_MOD * D_MODEL), 0.5 * D_MODEL ** -0.5),
        'b_mod': nrm(11, (L, N_MOD * D_MODEL), 0.01),
        'w_in': nrm(12, (L, D_MODEL, IN_COLS), D_MODEL ** -0.5),
        'w_out': nrm(13, (L, MIX_W, D_MODEL), MIX_W ** -0.5),
        'rpb': nrm(14, (L, HA, 2 * WIN_R - 1, 2 * WIN_C - 1), 0.2),
        'conv_w': nrm(15, (L, CONV_K, B_W), CONV_K ** -0.5),
        'conv_b': nrm(16, (L, B_W), 0.01),
        'conv_ln_g': 1.0 + nrm(17, (L, B_W), 0.1),
        'conv_ln_b': nrm(18, (L, B_W), 0.01),
        'rwkv_w0': nrm(19, (L, 2, C_W), 1.0),
        'rwkv_w_lora': nrm(20, (L, 2, W_RANK, C_W), W_RANK ** -0.5),
        'rwkv_a0': nrm(21, (L, 2, C_W), 0.5),
        'rwkv_a_lora': nrm(22, (L, 2, A_RANK, C_W), A_RANK ** -0.5),
        'rwkv_g_lora': nrm(23, (L, G_RANK, C_W), G_RANK ** -0.5),
        'rwkv_k_k': 1.0 + nrm(24, (L, C_W), 0.1),
        'rwkv_k_a': 1.0 + nrm(25, (L, C_W), 0.1),
        'rwkv_r_k': nrm(26, (L, HC, HEAD_DIM), 0.1),
        'rwkv_ln_g': 1.0 + nrm(27, (L, C_W), 0.1),
        'rwkv_ln_b': nrm(28, (L, C_W), 0.01),
        'w_up': nrm(29, (L, D_MODEL, 2 * D_FF), D_MODEL ** -0.5),
        'ffn_conv_w': nrm(30, (L, 3, D_FF), 3 ** -0.5),
        'ffn_conv_b': nrm(31, (L, D_FF), 0.01),
        'w_down': nrm(32, (L, D_FF, D_MODEL), D_FF ** -0.5),
        'final_g': 1.0 + nrm(33, (D_MODEL,), 0.1),
    }


def reference(x_prompt, x_sample, cache_k, cache_v, state_fwd, state_bwd, c, c_ctx,
              norm1_g, norm2_g, w_mod, b_mod, w_in, w_out, rpb, conv_w, conv_b, conv_ln_g, conv_ln_b,
              rwkv_w0, rwkv_w_lora, rwkv_a0, rwkv_a_lora, rwkv_g_lora, rwkv_k_k, rwkv_k_a, rwkv_r_k,
              rwkv_ln_g, rwkv_ln_b, w_up, ffn_conv_w, ffn_conv_b, w_down, final_g):
    xp, xs = x_prompt, x_sample
    c_prompt = c_ctx[None, :]
    zero_state = jnp.zeros((xp.shape[0], HC, HEAD_DIM, HEAD_DIM), xp.dtype)
    ks, vs, sfs, sbs = [], [], [], []
    for l in range(DEPTH):
        lp = {
            'n1': norm1_g[l], 'n2': norm2_g[l], 'w_mod': w_mod[l], 'b_mod': b_mod[l],
            'w_in': w_in[l], 'w_out': w_out[l],
            'conv_w': conv_w[l], 'conv_b': conv_b[l], 'conv_ln_g': conv_ln_g[l], 'conv_ln_b': conv_ln_b[l],
            'w0': rwkv_w0[l], 'w_lora': rwkv_w_lora[l], 'a0': rwkv_a0[l], 'a_lora': rwkv_a_lora[l],
            'g_lora': rwkv_g_lora[l], 'k_k': rwkv_k_k[l], 'k_a': rwkv_k_a[l], 'r_k': rwkv_r_k[l],
            'ln_g': rwkv_ln_g[l], 'ln_b': rwkv_ln_b[l],
            'w_up': w_up[l], 'ffn_conv_w': ffn_conv_w[l], 'ffn_conv_b': ffn_conv_b[l], 'w_down': w_down[l],
        }
        xp, k_ctx, v_ctx, s_f, s_b = trunk_layer(xp, c_prompt, context_attention, zero_state, zero_state, lp)
        ks.append(k_ctx)
        vs.append(v_ctx)
        sfs.append(s_f)
        sbs.append(s_b)
        attend = functools.partial(neighbourhood_attention, ctx_k=cache_k[:, l], ctx_v=cache_v[:, l], rpb=rpb[l])
        xs, _, _, _, _ = trunk_layer(xs, c, attend, state_fwd[:, l], state_bwd[:, l], lp)
    y_prompt = rmsnorm(xp, final_g)
    y_sample = rmsnorm(xs, final_g)
    new_cache_k = jnp.stack(ks, axis=1)
    new_cache_v = jnp.stack(vs, axis=1)
    new_state_fwd = jnp.stack(sfs, axis=1)
    new_state_bwd = jnp.stack(sbs, axis=1)
    return (y_prompt, y_sample, new_cache_k, new_cache_v, new_state_fwd, new_state_bwd)
```

```python
import functools

import numpy as np
import jax
import jax.numpy as jnp
from jax import lax
from jax.experimental import pallas as pl
from jax.experimental.pallas import tpu as pltpu

F32 = jnp.float32
BF16 = jnp.bfloat16

D_MODEL = 2048
DEPTH = 4
GRID_W = 64
HEAD_DIM = 64
A_W = 768
B_W = 512
C_W = 768
HA = A_W // HEAD_DIM
HC = C_W // HEAD_DIM
WIN_R = 8
WIN_C = 16
CONV_K = 31
W_RANK = 64
A_RANK = 64
G_RANK = 128
D_FF = 5632
N_MOD = 6
ATTN_SCALE = HEAD_DIM ** -0.5
DECAY_SCALE = 0.606531
RMS_EPS = 1e-6
LN_EPS = 1e-5
GN_EPS = 64e-5
L2_EPS = 1e-12
MASK_VALUE = -1e30

COL_Q, COL_K, COL_V = 0, 768, 1536
COL_R, COL_KC, COL_VC = 2304, 3072, 3840
COL_GA, COL_GB = 4608, 5120
COL_LORA = 5632
PROJ_COLS = 6144
LORA_W = 512

MOD_ROWS = 16
LANES = 128
CHAIN_GROUP = 64
V_SUB = HEAD_DIM // 2
VMEM_LIMIT = 56 * 1024 * 1024


def _cparams(*semantics):
    return pltpu.CompilerParams(dimension_semantics=semantics, vmem_limit_bytes=VMEM_LIMIT)


def _sigmoid(x):
    return jax.nn.sigmoid(x)


def _mod_kernel(c_ref, w_ref, b_ref, o_ref):
    c = c_ref[...]
    s = (c * _sigmoid(c)).astype(BF16)
    o_ref[0] = jnp.dot(s, w_ref[0].astype(BF16), preferred_element_type=F32) + b_ref[0]


def _modulation(cvecs, w_mod, b_mod):
    depth, d, n = w_mod.shape
    tn = 1024
    return pl.pallas_call(
        _mod_kernel,
        grid=(depth, n // tn),
        in_specs=[pl.BlockSpec((MOD_ROWS, d), lambda l, j: (0, 0)),
                  pl.BlockSpec((1, d, tn), lambda l, j: (l, 0, j)),
                  pl.BlockSpec((1, 1, tn), lambda l, j: (l, 0, j))],
        out_specs=pl.BlockSpec((1, MOD_ROWS, tn), lambda l, j: (l, 0, j)),
        out_shape=jax.ShapeDtypeStruct((depth, MOD_ROWS, n), F32),
        compiler_params=_cparams("arbitrary", "arbitrary"),
        name="modulation",
    )(cvecs, w_mod, b_mod.reshape(depth, 1, n))


def _norm_mod_kernel(x_ref, g_ref, sc_ref, sh_ref, o_ref):
    x = x_ref[...]
    inv = lax.rsqrt(jnp.mean(x * x, axis=-1, keepdims=True) + RMS_EPS)
    y = (x * inv) * g_ref[...]
    o_ref[...] = (y * (1.0 + sc_ref[0]) + sh_ref[0]).astype(o_ref.dtype)


def _norm_kernel(x_ref, g_ref, o_ref):
    x = x_ref[...]
    inv = lax.rsqrt(jnp.mean(x * x, axis=-1, keepdims=True) + RMS_EPS)
    o_ref[...] = (x * inv) * g_ref[...]


def _mod_spec(width, row0, row_stride, chunk, blocks_per_chunk=1, with_j=False):
    if with_j:
        return pl.BlockSpec((1, 1, width),
                            lambda b, i, j: (row0 + b * row_stride, 0, chunk * blocks_per_chunk + j))
    return pl.BlockSpec((1, 1, width), lambda b, i: (row0 + b * row_stride, 0, chunk))


def _norm_mod(x2d, gain, m3, row0, row_stride, chunk_shift, chunk_scale, bsz, t_len):
    tm = min(t_len, 512)
    nt = t_len // tm
    d = x2d.shape[1]
    return pl.pallas_call(
        _norm_mod_kernel,
        grid=(bsz, nt),
        in_specs=[pl.BlockSpec((tm, d), lambda b, i: (b * nt + i, 0)),
                  pl.BlockSpec((1, d), lambda b, i: (0, 0)),
                  _mod_spec(d, row0, row_stride, chunk_scale),
                  _mod_spec(d, row0, row_stride, chunk_shift)],
        out_specs=pl.BlockSpec((tm, d), lambda b, i: (b * nt + i, 0)),
        out_shape=jax.ShapeDtypeStruct(x2d.shape, BF16),
        compiler_params=_cparams("arbitrary", "arbitrary"),
        name="norm_mod",
    )(x2d, gain.reshape(1, d), m3, m3)


def _final_norm(x2d, gain):
    tm = 512
    m, d = x2d.shape
    return pl.pallas_call(
        _norm_kernel,
        grid=(m // tm,),
        in_specs=[pl.BlockSpec((tm, d), lambda i: (i, 0)),
                  pl.BlockSpec((1, d), lambda i: (0, 0))],
        out_specs=pl.BlockSpec((tm, d), lambda i: (i, 0)),
        out_shape=jax.ShapeDtypeStruct(x2d.shape, F32),
        compiler_params=_cparams("arbitrary"),
        name="final_norm",
    )(x2d, gain.reshape(1, d))


def _mm_kernel(a_ref, b_ref, o_ref):
    o_ref[...] = jnp.dot(a_ref[...], b_ref[...], preferred_element_type=F32).astype(o_ref.dtype)


def _mm_res_kernel(a_ref, b_ref, x_ref, g_ref, o_ref):
    y = jnp.dot(a_ref[...], b_ref[...], preferred_element_type=F32)
    o_ref[...] = x_ref[...] + g_ref[0] * y


def _matmul(a, b, tm, tn, out_dtype=F32):
    m, k = a.shape
    n = b.shape[1]
    return pl.pallas_call(
        _mm_kernel,
        grid=(m // tm, n // tn),
        in_specs=[pl.BlockSpec((tm, k), lambda i, j: (i, 0)),
                  pl.BlockSpec((k, tn), lambda i, j: (0, j))],
        out_specs=pl.BlockSpec((tm, tn), lambda i, j: (i, j)),
        out_shape=jax.ShapeDtypeStruct((m, n), out_dtype),
        compiler_params=_cparams("arbitrary", "arbitrary"),
        name="matmul",
    )(a, b)


def _matmul_gated_residual(a, b, x2d, m3, row0, row_stride, chunk, bsz, t_len, tm, tn):
    m, k = a.shape
    n = b.shape[1]
    tm = min(tm, t_len)
    nt = t_len // tm
    return pl.pallas_call(
        _mm_res_kernel,
        grid=(bsz, nt, n // tn),
        in_specs=[pl.BlockSpec((tm, k), lambda bb, i, j: (bb * nt + i, 0)),
                  pl.BlockSpec((k, tn), lambda bb, i, j: (0, j)),
                  pl.BlockSpec((tm, tn), lambda bb, i, j: (bb * nt + i, j)),
                  _mod_spec(tn, row0, row_stride, chunk, n // tn, with_j=True)],
        out_specs=pl.BlockSpec((tm, tn), lambda bb, i, j: (bb * nt + i, j)),
        out_shape=jax.ShapeDtypeStruct((m, n), F32),
        compiler_params=_cparams("arbitrary", "arbitrary", "arbitrary"),
        name="matmul_gated_residual",
    )(a, b, x2d, m3)


def _dot_nt(a, b):
    return lax.dot_general(a, b, (((1,), (1,)), ((), ())), preferred_element_type=F32)


def _ctx_attn_kernel(q_ref, k_ref, v_ref, o_ref):
    q = q_ref[...]
    k = k_ref[...].astype(BF16)
    v = v_ref[...].astype(BF16)
    lane = lax.broadcasted_iota(jnp.int32, (1, LANES), 1)
    outs = []
    for hh in range(2):
        qh = jnp.where((lane >= hh * HEAD_DIM) & (lane < (hh + 1) * HEAD_DIM), q, 0.0).astype(BF16)
        s = _dot_nt(qh, k) * ATTN_SCALE
        e = jnp.exp(s - jnp.max(s, axis=-1, keepdims=True))
        p = e * (1.0 / jnp.sum(e, axis=-1, keepdims=True))
        outs.append(jnp.dot(p.astype(BF16), v, preferred_element_type=F32))
    o_ref[...] = jnp.where(lane < HEAD_DIM, outs[0], outs[1]).astype(o_ref.dtype)


def _context_attention(proj, bsz, t_len):
    pairs = A_W // LANES
    return pl.pallas_call(
        _ctx_attn_kernel,
        grid=(bsz, pairs),
        in_specs=[pl.BlockSpec((t_len, LANES), lambda b, p: (b, COL_Q // LANES + p)),
                  pl.BlockSpec((t_len, LANES), lambda b, p: (b, COL_K // LANES + p)),
                  pl.BlockSpec((t_len, LANES), lambda b, p: (b, COL_V // LANES + p))],
        out_specs=pl.BlockSpec((t_len, LANES), lambda b, p: (b, p)),
        out_shape=jax.ShapeDtypeStruct((bsz * t_len, A_W), BF16),
        compiler_params=_cparams("arbitrary", "arbitrary"),
        name="context_attention",
    )(proj, proj, proj)


def _nbr_attn_kernel(q_ref, k_ref, v_ref, ck_ref, cv_ref, tbl_ref, o_ref, *, rows):
    i = pl.program_id(1)
    kr = min(WIN_R, rows)
    rs = jnp.clip(i - kr // 2, 0, rows - kr)
    start = pl.multiple_of(rs * GRID_W, GRID_W)
    nloc = kr * GRID_W
    lane = lax.broadcasted_iota(jnp.int32, (1, LANES), 1)
    for p in range(A_W // LANES):
        cols = slice(p * LANES, (p + 1) * LANES)
        qp = q_ref[:, cols]
        kp = k_ref[pl.ds(start, nloc), cols].astype(BF16)
        vp = v_ref[pl.ds(start, nloc), cols].astype(BF16)
        ckp = ck_ref[0, 0, :, cols].astype(BF16)
        cvp = cv_ref[0, 0, :, cols].astype(BF16)
        outs = []
        for hh in range(2):
            qh = jnp.where((lane >= hh * HEAD_DIM) & (lane < (hh + 1) * HEAD_DIM), qp, 0.0).astype(BF16)
            tbl = tbl_ref[0, 2 * p + hh]
            s_loc = jnp.where(tbl > 0.5 * MASK_VALUE, _dot_nt(qh, kp) * ATTN_SCALE + tbl, MASK_VALUE)
            s_ctx = _dot_nt(qh, ckp) * ATTN_SCALE
            mx = jnp.maximum(jnp.max(s_loc, axis=-1, keepdims=True), jnp.max(s_ctx, axis=-1, keepdims=True))
            e_loc = jnp.exp(s_loc - mx)
            e_ctx = jnp.exp(s_ctx - mx)
            inv = 1.0 / (jnp.sum(e_loc, axis=-1, keepdims=True) + jnp.sum(e_ctx, axis=-1, keepdims=True))
            outs.append(jnp.dot((e_loc * inv).astype(BF16), vp, preferred_element_type=F32)
                        + jnp.dot((e_ctx * inv).astype(BF16), cvp, preferred_element_type=F32))
        o_ref[:, cols] = jnp.where(lane < HEAD_DIM, outs[0], outs[1]).astype(o_ref.dtype)


def _bias_tables(rpb_l, rows):
    kr = min(WIN_R, rows)
    offs = sorted({i - int(np.clip(i - kr // 2, 0, rows - kr)) for i in range(rows)})
    qcol = np.arange(GRID_W)
    kcol = np.arange(GRID_W)
    win_start = np.clip(qcol - WIN_C // 2, 0, GRID_W - WIN_C)
    valid = (kcol[None, :] >= win_start[:, None]) & (kcol[None, :] < win_start[:, None] + WIN_C)
    col_idx = np.clip(kcol[None, :] - qcol[:, None] + WIN_C - 1, 0, 2 * WIN_C - 2)
    tables = []
    for off in offs:
        row_idx = np.arange(kr) - off + WIN_R - 1
        b = rpb_l.astype(F32)[:, row_idx][:, :, col_idx]
        b = jnp.where(valid[None, None], b, MASK_VALUE).transpose(0, 2, 1, 3)
        tables.append(b.reshape(rpb_l.shape[0], GRID_W, kr * GRID_W))
    return jnp.stack(tables, 0), offs


def _neighbourhood_attention(proj, cache_k4, cache_v4, layer, rpb_l, bsz, t_len):
    rows = t_len // GRID_W
    kr = min(WIN_R, rows)
    tables, offs = _bias_tables(rpb_l, rows)
    assert offs == list(range(len(offs)))
    past = cache_k4.shape[2]

    def variant(i):
        return i - jnp.clip(i - kr // 2, 0, rows - kr)

    return pl.pallas_call(
        functools.partial(_nbr_attn_kernel, rows=rows),
        grid=(bsz, rows),
        in_specs=[pl.BlockSpec((GRID_W, A_W), lambda b, i: (b * rows + i, COL_Q // A_W)),
                  pl.BlockSpec((t_len, A_W), lambda b, i: (b, COL_K // A_W)),
                  pl.BlockSpec((t_len, A_W), lambda b, i: (b, COL_V // A_W)),
                  pl.BlockSpec((1, 1, past, A_W), lambda b, i: (b, layer, 0, 0)),
                  pl.BlockSpec((1, 1, past, A_W), lambda b, i: (b, layer, 0, 0)),
                  pl.BlockSpec((1, HA, GRID_W, kr * GRID_W), lambda b, i: (variant(i), 0, 0, 0))],
        out_specs=pl.BlockSpec((GRID_W, A_W), lambda b, i: (b * rows + i, 0)),
        out_shape=jax.ShapeDtypeStruct((bsz * t_len, A_W), BF16),
        compiler_params=_cparams("arbitrary", "arbitrary"),
        name="neighbourhood_attention",
    )(proj, proj, proj, cache_k4, cache_v4, tables)


CONV_CHUNK = 64
CONV_PAD = 16


def _fill_padded(pad_ref, values, t_len):
    width = pad_ref.shape[1]
    pad_ref[0:CONV_PAD, :] = jnp.zeros((CONV_PAD, width), F32)
    pad_ref[CONV_PAD + t_len:2 * CONV_PAD + t_len, :] = jnp.zeros((CONV_PAD, width), F32)
    pad_ref[CONV_PAD:CONV_PAD + t_len, :] = values


def _conv_taps(pad_ref, w_ref, t0, taps):
    left = (taps - 1) // 2
    window = pad_ref[pl.ds(t0, CONV_CHUNK + 2 * CONV_PAD), :]
    acc = jnp.zeros((CONV_CHUNK, pad_ref.shape[1]), F32)
    for kk in range(taps):
        off = CONV_PAD - left + kk
        acc = acc + window[off:off + CONV_CHUNK, :] * w_ref[kk:kk + 1, :]
    return acc


def _conv_module_kernel(a_ref, b_ref, w_ref, cb_ref, g_ref, lb_ref, o_ref, pad_ref):
    t_len = a_ref.shape[0]
    _fill_padded(pad_ref, a_ref[...] * _sigmoid(b_ref[...]), t_len)

    def chunk(c, carry):
        t0 = pl.multiple_of(c * CONV_CHUNK, CONV_CHUNK)
        u = _conv_taps(pad_ref, w_ref, t0, CONV_K) + cb_ref[...]
        mu = jnp.mean(u, axis=-1, keepdims=True)
        dlt = u - mu
        var = jnp.mean(dlt * dlt, axis=-1, keepdims=True)
        y = dlt * lax.rsqrt(var + LN_EPS) * g_ref[...] + lb_ref[...]
        o_ref[pl.ds(t0, CONV_CHUNK), :] = (y * _sigmoid(y)).astype(o_ref.dtype)
        return carry

    lax.fori_loop(0, t_len // CONV_CHUNK, chunk, 0)


def _conv_module(proj, conv_w, conv_b, ln_g, ln_b, bsz, t_len):
    row = lambda v: v.reshape(1, B_W)
    return pl.pallas_call(
        _conv_module_kernel,
        grid=(bsz,),
        in_specs=[pl.BlockSpec((t_len, B_W), lambda b: (b, COL_GA // B_W)),
                  pl.BlockSpec((t_len, B_W), lambda b: (b, COL_GB // B_W)),
                  pl.BlockSpec((CONV_K, B_W), lambda b: (0, 0)),
                  pl.BlockSpec((1, B_W), lambda b: (0, 0)),
                  pl.BlockSpec((1, B_W), lambda b: (0, 0)),
                  pl.BlockSpec((1, B_W), lambda b: (0, 0))],
        out_specs=pl.BlockSpec((t_len, B_W), lambda b: (b, 0)),
        out_shape=jax.ShapeDtypeStruct((bsz * t_len, B_W), BF16),
        scratch_shapes=[pltpu.VMEM((t_len + 2 * CONV_PAD, B_W), F32)],
        compiler_params=_cparams("arbitrary"),
        name="conv_module",
    )(proj, proj, conv_w, row(conv_b), row(ln_g), row(ln_b))


def _ffn_act_kernel(u_ref, g_ref, w_ref, cb_ref, o_ref, pad_ref):
    t_len = u_ref.shape[0]
    _fill_padded(pad_ref, u_ref[...], t_len)

    def chunk(c, carry):
        t0 = pl.multiple_of(c * CONV_CHUNK, CONV_CHUNK)
        u = _conv_taps(pad_ref, w_ref, t0, 3) + cb_ref[...]
        o_ref[pl.ds(t0, CONV_CHUNK), :] = (u * _sigmoid(u) * g_ref[pl.ds(t0, CONV_CHUNK), :]).astype(o_ref.dtype)
        return carry

    lax.fori_loop(0, t_len // CONV_CHUNK, chunk, 0)


def _ffn_activation(up, dw_w, dw_b, bsz, t_len):
    tc = 512
    nc = D_FF // tc
    return pl.pallas_call(
        _ffn_act_kernel,
        grid=(bsz, nc),
        in_specs=[pl.BlockSpec((t_len, tc), lambda b, j: (b, j)),
                  pl.BlockSpec((t_len, tc), lambda b, j: (b, nc + j)),
                  pl.BlockSpec((3, tc), lambda b, j: (0, j)),
                  pl.BlockSpec((1, tc), lambda b, j: (0, j))],
        out_specs=pl.BlockSpec((t_len, tc), lambda b, j: (b, j)),
        out_shape=jax.ShapeDtypeStruct((bsz * t_len, D_FF), BF16),
        scratch_shapes=[pltpu.VMEM((t_len + 2 * CONV_PAD, tc), F32)],
        compiler_params=_cparams("arbitrary", "arbitrary"),
        name="ffn_activation",
    )(up, up, dw_w, dw_b.reshape(1, D_FF))


def _segment_sum(x, ones_blockdiag):
    hi = x.astype(BF16)
    lo = (x - hi.astype(F32)).astype(BF16)
    return (jnp.dot(hi, ones_blockdiag, preferred_element_type=F32)
            + jnp.dot(lo, ones_blockdiag, preferred_element_type=F32))


def _rwkv_prep_kernel(r_ref, k_ref, v_ref, s_ref, kkw_ref, ka_ref, rk_ref, w0_ref, a0_ref, wl_ref, al_ref,
                      gl_ref, e_ref, z_o, wf_o, kdf_o, bf_o, wb_o, kdb_o, bb_o, bonus_o, g_o):
    r = r_ref[...]
    k = k_ref[...]
    e = e_ref[...]
    kk = k * kkw_ref[...]
    kk = kk * lax.rsqrt(_segment_sum(kk * kk, e) + L2_EPS)
    xs = s_ref[...]
    xw = jnp.tanh(xs[:, 0:LANES]).astype(BF16)
    xa = xs[:, LANES:2 * LANES].astype(BF16)
    xg = _sigmoid(xs[:, 2 * LANES:3 * LANES]).astype(BF16)
    w_all = jnp.exp(-DECAY_SCALE * _sigmoid(w0_ref[...] + jnp.dot(xw, wl_ref[...], preferred_element_type=F32)))
    a_all = _sigmoid(a0_ref[...] + jnp.dot(xa, al_ref[...], preferred_element_type=F32))
    g_o[...] = jnp.dot(xg, gl_ref[...], preferred_element_type=F32)
    z_o[...] = -kk
    ka = ka_ref[...]
    kd_sum = None
    for d, (w_o, kd_o, b_o) in enumerate(((wf_o, kdf_o, bf_o), (wb_o, kdb_o, bb_o))):
        a = a_all[:, d * C_W:(d + 1) * C_W]
        kd = k * (1.0 + (a - 1.0) * ka)
        w_o[...] = w_all[:, d * C_W:(d + 1) * C_W]
        kd_o[...] = kd
        b_o[...] = kk * a
        kd_sum = kd if kd_sum is None else kd_sum + kd
    bonus_o[...] = _segment_sum(r * kd_sum * rk_ref[...], e) * v_ref[...]


def _rwkv_prep(proj, lp, ones_blockdiag):
    m = proj.shape[0]
    tm = 256
    tok = lambda col: pl.BlockSpec((tm, C_W), lambda i, c=col // C_W: (i, c))
    const = lambda shape: pl.BlockSpec(shape, lambda i: (0, 0))
    out = jax.ShapeDtypeStruct((m, C_W), F32)
    return pl.pallas_call(
        _rwkv_prep_kernel,
        grid=(m // tm,),
        in_specs=[tok(COL_R), tok(COL_KC), tok(COL_VC),
                  pl.BlockSpec((tm, LORA_W), lambda i: (i, COL_LORA // LORA_W)),
                  const((1, C_W)), const((1, C_W)), const((1, C_W)),
                  const((1, 2 * C_W)), const((1, 2 * C_W)),
                  const((LANES, 2 * C_W)), const((LANES, 2 * C_W)), const((LANES, C_W)),
                  const((C_W, C_W))],
        out_specs=[pl.BlockSpec((tm, C_W), lambda i: (i, 0))] * 9,
        out_shape=[out] * 9,
        compiler_params=_cparams("arbitrary"),
        name="rwkv_prep",
    )(proj, proj, proj, proj, lp['k_k'], lp['k_a'], lp['r_k'], lp['w0'], lp['a0'],
      lp['w_lora'], lp['a_lora'], lp['g_lora'], ones_blockdiag)


def _scan_kernel(z_ref, r_ref, w_ref, k_ref, b_ref, v_ref, s0_ref, o_ref, sf_ref, s_scr, *, tb):
    j = pl.program_id(1)

    def row(ref, t, kidx):
        return jnp.broadcast_to(ref[0, t, pl.ds(kidx, 1), :], (V_SUB, LANES))

    @pl.when(j == 0)
    def _():
        s_scr[...] = s0_ref[0]

    u0 = jnp.zeros((V_SUB, LANES), F32)
    for kidx in range(HEAD_DIM):
        u0 = u0 + s_scr[kidx] * row(z_ref, 0, kidx)

    def step(t, u):
        t_next = jnp.minimum(t + 1, tb - 1)
        vt = v_ref[0, t]
        o = jnp.zeros((V_SUB, LANES), F32)
        u_next = jnp.zeros((V_SUB, LANES), F32)
        for kidx in range(HEAD_DIM):
            s = s_scr[kidx] * row(w_ref, t, kidx) + u * row(b_ref, t, kidx) + vt * row(k_ref, t, kidx)
            s_scr[kidx] = s
            o = o + s * row(r_ref, t, kidx)
            u_next = u_next + s * row(z_ref, t_next, kidx)
        o_ref[0, t] = o
        return u_next

    lax.fori_loop(0, tb, step, u0)

    @pl.when(j == pl.num_programs(1) - 1)
    def _():
        sf_ref[0] = s_scr[...]


def _wkv_scan(z, r, w, k, b, v, s0, t_len):
    groups = z.shape[0]
    tb = 32
    vec = pl.BlockSpec((1, tb, HEAD_DIM, LANES), lambda g, j: (g, j, 0, 0))
    val = pl.BlockSpec((1, tb, V_SUB, LANES), lambda g, j: (g, j, 0, 0))
    state = pl.BlockSpec((1, HEAD_DIM, V_SUB, LANES), lambda g, j: (g, 0, 0, 0))
    return pl.pallas_call(
        functools.partial(_scan_kernel, tb=tb),
        grid=(groups, t_len // tb),
        in_specs=[vec, vec, vec, vec, vec, val, state],
        out_specs=[val, state],
        out_shape=[jax.ShapeDtypeStruct((groups, t_len, V_SUB, LANES), F32),
                   jax.ShapeDtypeStruct((groups, HEAD_DIM, V_SUB, LANES), F32)],
        scratch_shapes=[pltpu.VMEM((HEAD_DIM, V_SUB, LANES), F32)],
        compiler_params=_cparams("arbitrary", "arbitrary"),
        name="wkv_scan",
    )(z, r, w, k, b, v, s0)


def _rwkv_post_kernel(of_ref, ob_ref, bonus_ref, g_ref, lg_ref, lb_ref, e_ref, o_ref):
    e = e_ref[...]
    o = of_ref[...] + ob_ref[...]
    mu = _segment_sum(o, e) * (1.0 / HEAD_DIM)
    dlt = o - mu
    var = _segment_sum(dlt * dlt, e) * (1.0 / HEAD_DIM)
    y = dlt * lax.rsqrt(var + GN_EPS) * lg_ref[...] + lb_ref[...] + bonus_ref[...]
    o_ref[...] = (y * g_ref[...]).astype(o_ref.dtype)


def _rwkv_post(o_f, o_b, bonus, g, ln_g, ln_b, ones_blockdiag):
    m = o_f.shape[0]
    tm = 256
    tok = pl.BlockSpec((tm, C_W), lambda i: (i, 0))
    const = lambda shape: pl.BlockSpec(shape, lambda i: (0, 0))
    return pl.pallas_call(
        _rwkv_post_kernel,
        grid=(m // tm,),
        in_specs=[tok, tok, tok, tok, const((1, C_W)), const((1, C_W)), const((C_W, C_W))],
        out_specs=tok,
        out_shape=jax.ShapeDtypeStruct((m, C_W), BF16),
        compiler_params=_cparams("arbitrary"),
        name="rwkv_post",
    )(o_f, o_b, bonus, g, ln_g, ln_b, ones_blockdiag)


def _key_vectors_to_chains(x_f, x_b, bsz, t_len):
    x = jnp.stack([x_f.reshape(bsz, t_len, HC, HEAD_DIM), x_b.reshape(bsz, t_len, HC, HEAD_DIM)[:, ::-1]], 0)
    groups = 2 * bsz * HC // CHAIN_GROUP
    x = x.transpose(2, 4, 0, 1, 3).reshape(t_len, HEAD_DIM, groups, CHAIN_GROUP).transpose(2, 0, 1, 3)
    return jnp.concatenate([x, x], axis=-1)


def _values_to_chains(v, bsz, t_len):
    v = v.reshape(bsz, t_len, HC, 2, V_SUB)
    x = jnp.stack([v, v[:, ::-1]], 0)
    groups = 2 * bsz * HC // CHAIN_GROUP
    x = x.transpose(2, 5, 4, 0, 1, 3).reshape(t_len, V_SUB, 2, groups, CHAIN_GROUP)
    return x.transpose(3, 0, 1, 2, 4).reshape(groups, t_len, V_SUB, LANES)


def _values_from_chains(o, bsz, t_len):
    groups = o.shape[0]
    x = o.reshape(groups, t_len, V_SUB, 2, CHAIN_GROUP).transpose(1, 2, 3, 0, 4)
    x = x.reshape(t_len, V_SUB, 2, 2, bsz, HC).transpose(3, 4, 0, 5, 2, 1)
    x = x.reshape(2, bsz, t_len, C_W)
    return x[0].reshape(bsz * t_len, C_W), x[1][:, ::-1].reshape(bsz * t_len, C_W)


def _states_to_chains(s_f, s_b, bsz):
    x = jnp.stack([s_f, s_b], 0).astype(F32).reshape(2, bsz, HC, 2, V_SUB, HEAD_DIM)
    groups = 2 * bsz * HC // CHAIN_GROUP
    x = x.transpose(5, 4, 3, 0, 1, 2).reshape(HEAD_DIM, V_SUB, 2, groups, CHAIN_GROUP)
    return x.transpose(3, 0, 1, 2, 4).reshape(groups, HEAD_DIM, V_SUB, LANES)


def _states_from_chains(s, bsz):
    groups = s.shape[0]
    x = s.reshape(groups, HEAD_DIM, V_SUB, 2, CHAIN_GROUP).transpose(1, 2, 3, 0, 4)
    x = x.reshape(HEAD_DIM, V_SUB, 2, 2, bsz, HC).transpose(3, 4, 5, 2, 1, 0)
    x = x.reshape(2, bsz, HC, HEAD_DIM, HEAD_DIM)
    return x[0], x[1]


def _rwkv7(proj, lp, ones_blockdiag, s0_f, s0_b, bsz, t_len):
    z, w_f, kd_f, b_f, w_b, kd_b, b_b, bonus, g = _rwkv_prep(proj, lp, ones_blockdiag)
    r = proj[:, COL_R:COL_R + C_W]
    v = proj[:, COL_VC:COL_VC + C_W]
    to_chains = functools.partial(_key_vectors_to_chains, bsz=bsz, t_len=t_len)
    o, s_fin = _wkv_scan(to_chains(z, z), to_chains(r, r), to_chains(w_f, w_b), to_chains(kd_f, kd_b),
                         to_chains(b_f, b_b), _values_to_chains(v, bsz, t_len),
                         _states_to_chains(s0_f, s0_b, bsz), t_len)
    o_f, o_b = _values_from_chains(o, bsz, t_len)
    y_c = _rwkv_post(o_f, o_b, bonus, g, lp['ln_g'], lp['ln_b'], ones_blockdiag)
    s_f, s_b = _states_from_chains(s_fin, bsz)
    return y_c, s_f, s_b


def _trunk_layer(x2d, bsz, t_len, m3, row0, row_stride, attend, s0_f, s0_b, lp, ones_blockdiag):
    mod = dict(m3=m3, row0=row0, row_stride=row_stride, bsz=bsz, t_len=t_len)
    h = _norm_mod(x2d, lp['n1'], chunk_shift=0, chunk_scale=1, **mod)
    proj = _matmul(h, lp['w_in'], tm=min(512, x2d.shape[0]), tn=1024)
    y_a = attend(proj)
    y_b = _conv_module(proj, lp['conv_w'], lp['conv_b'], lp['conv_ln_g'], lp['conv_ln_b'], bsz, t_len)
    y_c, s_f, s_b = _rwkv7(proj, lp, ones_blockdiag, s0_f, s0_b, bsz, t_len)
    y = jnp.concatenate([y_a, y_b, y_c], axis=-1)
    x2d = _matmul_gated_residual(y, lp['w_out'], x2d, chunk=2, tm=512, tn=1024, **mod)
    h2 = _norm_mod(x2d, lp['n2'], chunk_shift=3, chunk_scale=4, **mod)
    up = _matmul(h2, lp['w_up'], tm=min(512, x2d.shape[0]), tn=1024)
    act = _ffn_activation(up, lp['ffn_conv_w'], lp['ffn_conv_b'], bsz, t_len)
    x2d = _matmul_gated_residual(act, lp['w_down'], x2d, chunk=5, tm=512, tn=512, **mod)
    return x2d, proj, s_f, s_b


def _permute_w_in(w_in_l):
    sizes = (A_W, A_W, A_W, B_W, B_W, C_W, C_W, C_W, W_RANK, W_RANK, A_RANK, A_RANK, G_RANK)
    starts = np.concatenate([[0], np.cumsum(sizes)])
    part = lambda i: w_in_l[:, starts[i]:starts[i + 1]]
    q, k, v, ga, gb, r, kc, vc, xwf, xwb, xaf, xab, xg = (part(i) for i in range(len(sizes)))
    pad = jnp.zeros((w_in_l.shape[0], PROJ_COLS - int(starts[-1])), w_in_l.dtype)
    return jnp.concatenate([q, k, v, r, kc, vc, ga, gb, xwf, xwb, xaf, xab, xg, pad], axis=1).astype(BF16)


def _padded_lora(lora_fb, rank):
    z = jnp.zeros((rank, C_W), lora_fb.dtype)
    top = jnp.concatenate([lora_fb[0], z], axis=1)
    bot = jnp.concatenate([z, lora_fb[1]], axis=1)
    return jnp.concatenate([top, bot], axis=0).astype(BF16)


def kernel(x_prompt, x_sample, cache_k, cache_v, state_fwd, state_bwd, c, c_ctx, norm1_g, norm2_g, w_mod, b_mod, w_in, w_out, rpb, conv_w, conv_b, conv_ln_g, conv_ln_b, rwkv_w0, rwkv_w_lora, rwkv_a0, rwkv_a_lora, rwkv_g_lora, rwkv_k_k, rwkv_k_a, rwkv_r_k, rwkv_ln_g, rwkv_ln_b, w_up, ffn_conv_w, ffn_conv_b, w_down, final_g):
    bp, tp, d = x_prompt.shape
    bs, ts, _ = x_sample.shape
    depth = w_in.shape[0]
    past = cache_k.shape[2]
    assert bs + 1 <= MOD_ROWS

    cvecs = jnp.concatenate([c_ctx[None, :], c, jnp.zeros((MOD_ROWS - 1 - bs, d), F32)], axis=0)
    m3 = _modulation(cvecs, w_mod, b_mod).reshape(depth * MOD_ROWS, 1, N_MOD * d)

    head_of_lane = np.arange(C_W) // HEAD_DIM
    ones_blockdiag = jnp.asarray(head_of_lane[:, None] == head_of_lane[None, :], BF16)
    cache_k4 = cache_k.reshape(bs, depth, past, A_W)
    cache_v4 = cache_v.reshape(bs, depth, past, A_W)
    zero_state = jnp.zeros((bp, HC, HEAD_DIM, HEAD_DIM), F32)

    xp = x_prompt.reshape(bp * tp, d)
    xs = x_sample.reshape(bs * ts, d)
    ks, vs, sfs, sbs = [], [], [], []
    for l in range(depth):
        row = lambda v: v.reshape(1, -1)
        lp = {
            'n1': norm1_g[l], 'n2': norm2_g[l],
            'w_in': _permute_w_in(w_in[l]), 'w_out': w_out[l].astype(BF16),
            'conv_w': conv_w[l], 'conv_b': conv_b[l], 'conv_ln_g': conv_ln_g[l], 'conv_ln_b': conv_ln_b[l],
            'w0': rwkv_w0[l].reshape(1, 2 * C_W), 'a0': rwkv_a0[l].reshape(1, 2 * C_W),
            'w_lora': _padded_lora(rwkv_w_lora[l], W_RANK), 'a_lora': _padded_lora(rwkv_a_lora[l], A_RANK),
            'g_lora': rwkv_g_lora[l].astype(BF16),
            'k_k': row(rwkv_k_k[l]), 'k_a': row(rwkv_k_a[l]), 'r_k': row(rwkv_r_k[l]),
            'ln_g': row(rwkv_ln_g[l]), 'ln_b': row(rwkv_ln_b[l]),
            'w_up': w_up[l].astype(BF16), 'ffn_conv_w': ffn_conv_w[l], 'ffn_conv_b': ffn_conv_b[l],
            'w_down': w_down[l].astype(BF16),
        }
        ctx_attend = functools.partial(_context_attention, bsz=bp, t_len=tp)
        xp, proj_p, s_f, s_b = _trunk_layer(xp, bp, tp, m3, l * MOD_ROWS, 0, ctx_attend,
                                            zero_state, zero_state, lp, ones_blockdiag)
        ks.append(proj_p[:, COL_K:COL_K + A_W].reshape(bp, tp, HA, HEAD_DIM))
        vs.append(proj_p[:, COL_V:COL_V + A_W].reshape(bp, tp, HA, HEAD_DIM))
        sfs.append(s_f)
        sbs.append(s_b)
        nbr_attend = functools.partial(_neighbourhood_attention, cache_k4=cache_k4, cache_v4=cache_v4,
                                       layer=l, rpb_l=rpb[l], bsz=bs, t_len=ts)
        xs, _, _, _ = _trunk_layer(xs, bs, ts, m3, l * MOD_ROWS + 1, 1, nbr_attend,
                                   state_fwd[:, l], state_bwd[:, l], lp, ones_blockdiag)
    y_prompt = _final_norm(xp, final_g).reshape(bp, tp, d)
    y_sample = _final_norm(xs, final_g).reshape(bs, ts, d)
    return (y_prompt, y_sample, jnp.stack(ks, axis=1), jnp.stack(vs, axis=1),
            jnp.stack(sfs, axis=1), jnp.stack(sbs, axis=1))
```

```python
import functools

import numpy as np
import jax
import jax.numpy as jnp
from jax import lax
from jax.experimental import pallas as pl
from jax.experimental.pallas import tpu as pltpu

F32 = jnp.float32
BF16 = jnp.bfloat16

D_MODEL = 2048
DEPTH = 4
GRID_W = 64
HEAD_DIM = 64
A_W = 768
B_W = 512
C_W = 768
HA = A_W // HEAD_DIM
HC = C_W // HEAD_DIM
WIN_R = 8
WIN_C = 16
CONV_K = 31
W_RANK = 64
A_RANK = 64
G_RANK = 128
D_FF = 5632
N_MOD = 6
ATTN_SCALE = HEAD_DIM ** -0.5
DECAY_SCALE = 0.606531
RMS_EPS = 1e-6
LN_EPS = 1e-5
GN_EPS = 64e-5
L2_EPS = 1e-12
MASK_VALUE = -1e30

COL_Q, COL_K, COL_V = 0, 768, 1536
COL_R, COL_KC, COL_VC = 2304, 3072, 3840
COL_GA, COL_GB = 4608, 5120
COL_LORA = 5632
PROJ_COLS = 6144
LORA_W = 512

MOD_ROWS = 16
LANES = 128
CHAIN_GROUP = 64
VMEM_LIMIT = 56 * 1024 * 1024


def _cparams(*semantics):
    return pltpu.CompilerParams(dimension_semantics=semantics, vmem_limit_bytes=VMEM_LIMIT)


def _sigmoid(x):
    return jax.nn.sigmoid(x)


def _mod_kernel(c_ref, w_ref, b_ref, o_ref):
    c = c_ref[...]
    s = (c * _sigmoid(c)).astype(BF16)
    o_ref[0] = jnp.dot(s, w_ref[0].astype(BF16), preferred_element_type=F32) + b_ref[0]


def _modulation(cvecs, w_mod, b_mod):
    depth, d, n = w_mod.shape
    tn = 1024
    return pl.pallas_call(
        _mod_kernel,
        grid=(depth, n // tn),
        in_specs=[pl.BlockSpec((MOD_ROWS, d), lambda l, j: (0, 0)),
                  pl.BlockSpec((1, d, tn), lambda l, j: (l, 0, j)),
                  pl.BlockSpec((1, 1, tn), lambda l, j: (l, 0, j))],
        out_specs=pl.BlockSpec((1, MOD_ROWS, tn), lambda l, j: (l, 0, j)),
        out_shape=jax.ShapeDtypeStruct((depth, MOD_ROWS, n), F32),
        compiler_params=_cparams("arbitrary", "arbitrary"),
        name="modulation",
    )(cvecs, w_mod, b_mod.reshape(depth, 1, n))


def _norm_mod_kernel(x_ref, g_ref, sc_ref, sh_ref, o_ref):
    x = x_ref[...]
    inv = lax.rsqrt(jnp.mean(x * x, axis=-1, keepdims=True) + RMS_EPS)
    y = (x * inv) * g_ref[...]
    o_ref[...] = (y * (1.0 + sc_ref[0]) + sh_ref[0]).astype(o_ref.dtype)


def _norm_kernel(x_ref, g_ref, o_ref):
    x = x_ref[...]
    inv = lax.rsqrt(jnp.mean(x * x, axis=-1, keepdims=True) + RMS_EPS)
    o_ref[...] = (x * inv) * g_ref[...]


def _mod_spec(width, row0, row_stride, chunk, blocks_per_chunk=1, with_j=False):
    if with_j:
        return pl.BlockSpec((1, 1, width),
                            lambda b, i, j: (row0 + b * row_stride, 0, chunk * blocks_per_chunk + j))
    return pl.BlockSpec((1, 1, width), lambda b, i: (row0 + b * row_stride, 0, chunk))


def _norm_mod(x2d, gain, m3, row0, row_stride, chunk_shift, chunk_scale, bsz, t_len):
    tm = min(t_len, 512)
    nt = t_len // tm
    d = x2d.shape[1]
    return pl.pallas_call(
        _norm_mod_kernel,
        grid=(bsz, nt),
        in_specs=[pl.BlockSpec((tm, d), lambda b, i: (b * nt + i, 0)),
                  pl.BlockSpec((1, d), lambda b, i: (0, 0)),
                  _mod_spec(d, row0, row_stride, chunk_scale),
                  _mod_spec(d, row0, row_stride, chunk_shift)],
        out_specs=pl.BlockSpec((tm, d), lambda b, i: (b * nt + i, 0)),
        out_shape=jax.ShapeDtypeStruct(x2d.shape, BF16),
        compiler_params=_cparams("arbitrary", "arbitrary"),
        name="norm_mod",
    )(x2d, gain.reshape(1, d), m3, m3)


def _final_norm(x2d, gain):
    tm = 512
    m, d = x2d.shape
    return pl.pallas_call(
        _norm_kernel,
        grid=(m // tm,),
        in_specs=[pl.BlockSpec((tm, d), lambda i: (i, 0)),
                  pl.BlockSpec((1, d), lambda i: (0, 0))],
        out_specs=pl.BlockSpec((tm, d), lambda i: (i, 0)),
        out_shape=jax.ShapeDtypeStruct(x2d.shape, F32),
        compiler_params=_cparams("arbitrary"),
        name="final_norm",
    )(x2d, gain.reshape(1, d))


def _mm_kernel(a_ref, b_ref, o_ref):
    o_ref[...] = jnp.dot(a_ref[...], b_ref[...], preferred_element_type=F32).astype(o_ref.dtype)


def _mm_res_kernel(a_ref, b_ref, x_ref, g_ref, o_ref):
    y = jnp.dot(a_ref[...], b_ref[...], preferred_element_type=F32)
    o_ref[...] = x_ref[...] + g_ref[0] * y


def _matmul(a, b, tm, tn, out_dtype=F32):
    m, k = a.shape
    n = b.shape[1]
    return pl.pallas_call(
        _mm_kernel,
        grid=(m // tm, n // tn),
        in_specs=[pl.BlockSpec((tm, k), lambda i, j: (i, 0)),
                  pl.BlockSpec((k, tn), lambda i, j: (0, j))],
        out_specs=pl.BlockSpec((tm, tn), lambda i, j: (i, j)),
        out_shape=jax.ShapeDtypeStruct((m, n), out_dtype),
        compiler_params=_cparams("arbitrary", "arbitrary"),
        name="matmul",
    )(a, b)


def _matmul_gated_residual(a, b, x2d, m3, row0, row_stride, chunk, bsz, t_len, tm, tn):
    m, k = a.shape
    n = b.shape[1]
    tm = min(tm, t_len)
    nt = t_len // tm
    return pl.pallas_call(
        _mm_res_kernel,
        grid=(bsz, nt, n // tn),
        in_specs=[pl.BlockSpec((tm, k), lambda bb, i, j: (bb * nt + i, 0)),
                  pl.BlockSpec((k, tn), lambda bb, i, j: (0, j)),
                  pl.BlockSpec((tm, tn), lambda bb, i, j: (bb * nt + i, j)),
                  _mod_spec(tn, row0, row_stride, chunk, n // tn, with_j=True)],
        out_specs=pl.BlockSpec((tm, tn), lambda bb, i, j: (bb * nt + i, j)),
        out_shape=jax.ShapeDtypeStruct((m, n), F32),
        compiler_params=_cparams("arbitrary", "arbitrary", "arbitrary"),
        name="matmul_gated_residual",
    )(a, b, x2d, m3)


def _dot_nt(a, b):
    return lax.dot_general(a, b, (((1,), (1,)), ((), ())), preferred_element_type=F32)


def _ctx_attn_kernel(q_ref, k_ref, v_ref, o_ref):
    q = q_ref[...]
    k = k_ref[...].astype(BF16)
    v = v_ref[...].astype(BF16)
    lane = lax.broadcasted_iota(jnp.int32, (1, LANES), 1)
    outs = []
    for hh in range(2):
        qh = jnp.where((lane >= hh * HEAD_DIM) & (lane < (hh + 1) * HEAD_DIM), q, 0.0).astype(BF16)
        s = _dot_nt(qh, k) * ATTN_SCALE
        e = jnp.exp(s - jnp.max(s, axis=-1, keepdims=True))
        p = e * (1.0 / jnp.sum(e, axis=-1, keepdims=True))
        outs.append(jnp.dot(p.astype(BF16), v, preferred_element_type=F32))
    o_ref[...] = jnp.where(lane < HEAD_DIM, outs[0], outs[1]).astype(o_ref.dtype)


def _context_attention(proj, bsz, t_len):
    pairs = A_W // LANES
    return pl.pallas_call(
        _ctx_attn_kernel,
        grid=(bsz, pairs),
        in_specs=[pl.BlockSpec((t_len, LANES), lambda b, p: (b, COL_Q // LANES + p)),
                  pl.BlockSpec((t_len, LANES), lambda b, p: (b, COL_K // LANES + p)),
                  pl.BlockSpec((t_len, LANES), lambda b, p: (b, COL_V // LANES + p))],
        out_specs=pl.BlockSpec((t_len, LANES), lambda b, p: (b, p)),
        out_shape=jax.ShapeDtypeStruct((bsz * t_len, A_W), BF16),
        compiler_params=_cparams("arbitrary", "arbitrary"),
        name="context_attention",
    )(proj, proj, proj)


def _nbr_attn_kernel(q_ref, k_ref, v_ref, ck_ref, cv_ref, tbl_ref, o_ref, *, rows):
    i = pl.program_id(1)
    kr = min(WIN_R, rows)
    rs = jnp.clip(i - kr // 2, 0, rows - kr)
    start = pl.multiple_of(rs * GRID_W, GRID_W)
    nloc = kr * GRID_W
    lane = lax.broadcasted_iota(jnp.int32, (1, LANES), 1)
    for p in range(A_W // LANES):
        cols = slice(p * LANES, (p + 1) * LANES)
        qp = q_ref[:, cols]
        kp = k_ref[pl.ds(start, nloc), cols].astype(BF16)
        vp = v_ref[pl.ds(start, nloc), cols].astype(BF16)
        ckp = ck_ref[0, 0, :, cols].astype(BF16)
        cvp = cv_ref[0, 0, :, cols].astype(BF16)
        outs = []
        for hh in range(2):
            qh = jnp.where((lane >= hh * HEAD_DIM) & (lane < (hh + 1) * HEAD_DIM), qp, 0.0).astype(BF16)
            tbl = tbl_ref[0, 2 * p + hh]
            s_loc = jnp.where(tbl > 0.5 * MASK_VALUE, _dot_nt(qh, kp) * ATTN_SCALE + tbl, MASK_VALUE)
            s_ctx = _dot_nt(qh, ckp) * ATTN_SCALE
            mx = jnp.maximum(jnp.max(s_loc, axis=-1, keepdims=True), jnp.max(s_ctx, axis=-1, keepdims=True))
            e_loc = jnp.exp(s_loc - mx)
            e_ctx = jnp.exp(s_ctx - mx)
            inv = 1.0 / (jnp.sum(e_loc, axis=-1, keepdims=True) + jnp.sum(e_ctx, axis=-1, keepdims=True))
            outs.append(jnp.dot((e_loc * inv).astype(BF16), vp, preferred_element_type=F32)
                        + jnp.dot((e_ctx * inv).astype(BF16), cvp, preferred_element_type=F32))
        o_ref[:, cols] = jnp.where(lane < HEAD_DIM, outs[0], outs[1]).astype(o_ref.dtype)


def _bias_tables(rpb_l, rows):
    kr = min(WIN_R, rows)
    offs = sorted({i - int(np.clip(i - kr // 2, 0, rows - kr)) for i in range(rows)})
    qcol = np.arange(GRID_W)
    kcol = np.arange(GRID_W)
    win_start = np.clip(qcol - WIN_C // 2, 0, GRID_W - WIN_C)
    valid = (kcol[None, :] >= win_start[:, None]) & (kcol[None, :] < win_start[:, None] + WIN_C)
    col_idx = np.clip(kcol[None, :] - qcol[:, None] + WIN_C - 1, 0, 2 * WIN_C - 2)
    tables = []
    for off in offs:
        row_idx = np.arange(kr) - off + WIN_R - 1
        b = rpb_l.astype(F32)[:, row_idx][:, :, col_idx]
        b = jnp.where(valid[None, None], b, MASK_VALUE).transpose(0, 2, 1, 3)
        tables.append(b.reshape(rpb_l.shape[0], GRID_W, kr * GRID_W))
    return jnp.stack(tables, 0), offs


def _neighbourhood_attention(proj, cache_k4, cache_v4, layer, rpb_l, bsz, t_len):
    rows = t_len // GRID_W
    kr = min(WIN_R, rows)
    tables, offs = _bias_tables(rpb_l, rows)
    assert offs == list(range(len(offs)))
    past = cache_k4.shape[2]

    def variant(i):
        return i - jnp.clip(i - kr // 2, 0, rows - kr)

    return pl.pallas_call(
        functools.partial(_nbr_attn_kernel, rows=rows),
        grid=(bsz, rows),
        in_specs=[pl.BlockSpec((GRID_W, A_W), lambda b, i: (b * rows + i, COL_Q // A_W)),
                  pl.BlockSpec((t_len, A_W), lambda b, i: (b, COL_K // A_W)),
                  pl.BlockSpec((t_len, A_W), lambda b, i: (b, COL_V // A_W)),
                  pl.BlockSpec((1, 1, past, A_W), lambda b, i: (b, layer, 0, 0)),
                  pl.BlockSpec((1, 1, past, A_W), lambda b, i: (b, layer, 0, 0)),
                  pl.BlockSpec((1, HA, GRID_W, kr * GRID_W), lambda b, i: (variant(i), 0, 0, 0))],
        out_specs=pl.BlockSpec((GRID_W, A_W), lambda b, i: (b * rows + i, 0)),
        out_shape=jax.ShapeDtypeStruct((bsz * t_len, A_W), BF16),
        compiler_params=_cparams("arbitrary", "arbitrary"),
        name="neighbourhood_attention",
    )(proj, proj, proj, cache_k4, cache_v4, tables)


CONV_CHUNK = 64
CONV_PAD = 16


def _fill_padded(pad_ref, values, t_len):
    width = pad_ref.shape[1]
    pad_ref[0:CONV_PAD, :] = jnp.zeros((CONV_PAD, width), F32)
    pad_ref[CONV_PAD + t_len:2 * CONV_PAD + t_len, :] = jnp.zeros((CONV_PAD, width), F32)
    pad_ref[CONV_PAD:CONV_PAD + t_len, :] = values


def _conv_taps(pad_ref, w_ref, t0, taps):
    left = (taps - 1) // 2
    window = pad_ref[pl.ds(t0, CONV_CHUNK + 2 * CONV_PAD), :]
    acc = jnp.zeros((CONV_CHUNK, pad_ref.shape[1]), F32)
    for kk in range(taps):
        off = CONV_PAD - left + kk
        acc = acc + window[off:off + CONV_CHUNK, :] * w_ref[kk:kk + 1, :]
    return acc


def _conv_module_kernel(a_ref, b_ref, w_ref, cb_ref, g_ref, lb_ref, o_ref, pad_ref):
    t_len = a_ref.shape[0]
    _fill_padded(pad_ref, a_ref[...] * _sigmoid(b_ref[...]), t_len)

    def chunk(c, carry):
        t0 = pl.multiple_of(c * CONV_CHUNK, CONV_CHUNK)
        u = _conv_taps(pad_ref, w_ref, t0, CONV_K) + cb_ref[...]
        mu = jnp.mean(u, axis=-1, keepdims=True)
        dlt = u - mu
        var = jnp.mean(dlt * dlt, axis=-1, keepdims=True)
        y = dlt * lax.rsqrt(var + LN_EPS) * g_ref[...] + lb_ref[...]
        o_ref[pl.ds(t0, CONV_CHUNK), :] = (y * _sigmoid(y)).astype(o_ref.dtype)
        return carry

    lax.fori_loop(0, t_len // CONV_CHUNK, chunk, 0)


def _conv_module(proj, conv_w, conv_b, ln_g, ln_b, bsz, t_len):
    row = lambda v: v.reshape(1, B_W)
    return pl.pallas_call(
        _conv_module_kernel,
        grid=(bsz,),
        in_specs=[pl.BlockSpec((t_len, B_W), lambda b: (b, COL_GA // B_W)),
                  pl.BlockSpec((t_len, B_W), lambda b: (b, COL_GB // B_W)),
                  pl.BlockSpec((CONV_K, B_W), lambda b: (0, 0)),
                  pl.BlockSpec((1, B_W), lambda b: (0, 0)),
                  pl.BlockSpec((1, B_W), lambda b: (0, 0)),
                  pl.BlockSpec((1, B_W), lambda b: (0, 0))],
        out_specs=pl.BlockSpec((t_len, B_W), lambda b: (b, 0)),
        out_shape=jax.ShapeDtypeStruct((bsz * t_len, B_W), BF16),
        scratch_shapes=[pltpu.VMEM((t_len + 2 * CONV_PAD, B_W), F32)],
        compiler_params=_cparams("arbitrary"),
        name="conv_module",
    )(proj, proj, conv_w, row(conv_b), row(ln_g), row(ln_b))


def _ffn_act_kernel(u_ref, g_ref, w_ref, cb_ref, o_ref, pad_ref):
    t_len = u_ref.shape[0]
    _fill_padded(pad_ref, u_ref[...], t_len)

    def chunk(c, carry):
        t0 = pl.multiple_of(c * CONV_CHUNK, CONV_CHUNK)
        u = _conv_taps(pad_ref, w_ref, t0, 3) + cb_ref[...]
        o_ref[pl.ds(t0, CONV_CHUNK), :] = (u * _sigmoid(u) * g_ref[pl.ds(t0, CONV_CHUNK), :]).astype(o_ref.dtype)
        return carry

    lax.fori_loop(0, t_len // CONV_CHUNK, chunk, 0)


def _ffn_activation(up, dw_w, dw_b, bsz, t_len):
    tc = 512
    nc = D_FF // tc
    return pl.pallas_call(
        _ffn_act_kernel,
        grid=(bsz, nc),
        in_specs=[pl.BlockSpec((t_len, tc), lambda b, j: (b, j)),
                  pl.BlockSpec((t_len, tc), lambda b, j: (b, nc + j)),
                  pl.BlockSpec((3, tc), lambda b, j: (0, j)),
                  pl.BlockSpec((1, tc), lambda b, j: (0, j))],
        out_specs=pl.BlockSpec((t_len, tc), lambda b, j: (b, j)),
        out_shape=jax.ShapeDtypeStruct((bsz * t_len, D_FF), BF16),
        scratch_shapes=[pltpu.VMEM((t_len + 2 * CONV_PAD, tc), F32)],
        compiler_params=_cparams("arbitrary", "arbitrary"),
        name="ffn_activation",
    )(up, up, dw_w, dw_b.reshape(1, D_FF))


def _segment_sum(x, ones_blockdiag):
    hi = x.astype(BF16)
    lo = (x - hi.astype(F32)).astype(BF16)
    return (jnp.dot(hi, ones_blockdiag, preferred_element_type=F32)
            + jnp.dot(lo, ones_blockdiag, preferred_element_type=F32))


def _rwkv_prep_kernel(r_ref, k_ref, v_ref, s_ref, kkw_ref, ka_ref, rk_ref, w0_ref, a0_ref, wl_ref, al_ref,
                      gl_ref, e_ref, z_o, wf_o, kdf_o, bf_o, wb_o, kdb_o, bb_o, bonus_o, g_o):
    r = r_ref[...]
    k = k_ref[...]
    e = e_ref[...]
    kk = k * kkw_ref[...]
    kk = kk * lax.rsqrt(_segment_sum(kk * kk, e) + L2_EPS)
    xs = s_ref[...]
    xw = jnp.tanh(xs[:, 0:LANES]).astype(BF16)
    xa = xs[:, LANES:2 * LANES].astype(BF16)
    xg = _sigmoid(xs[:, 2 * LANES:3 * LANES]).astype(BF16)
    w_all = jnp.exp(-DECAY_SCALE * _sigmoid(w0_ref[...] + jnp.dot(xw, wl_ref[...], preferred_element_type=F32)))
    a_all = _sigmoid(a0_ref[...] + jnp.dot(xa, al_ref[...], preferred_element_type=F32))
    g_o[...] = jnp.dot(xg, gl_ref[...], preferred_element_type=F32)
    z_o[...] = -kk
    ka = ka_ref[...]
    kd_sum = None
    for d, (w_o, kd_o, b_o) in enumerate(((wf_o, kdf_o, bf_o), (wb_o, kdb_o, bb_o))):
        a = a_all[:, d * C_W:(d + 1) * C_W]
        kd = k * (1.0 + (a - 1.0) * ka)
        w_o[...] = w_all[:, d * C_W:(d + 1) * C_W]
        kd_o[...] = kd
        b_o[...] = kk * a
        kd_sum = kd if kd_sum is None else kd_sum + kd
    bonus_o[...] = _segment_sum(r * kd_sum * rk_ref[...], e) * v_ref[...]


def _rwkv_prep(proj, lp, ones_blockdiag):
    m = proj.shape[0]
    tm = 256
    tok = lambda col: pl.BlockSpec((tm, C_W), lambda i, c=col // C_W: (i, c))
    const = lambda shape: pl.BlockSpec(shape, lambda i: (0, 0))
    out = jax.ShapeDtypeStruct((m, C_W), F32)
    return pl.pallas_call(
        _rwkv_prep_kernel,
        grid=(m // tm,),
        in_specs=[tok(COL_R), tok(COL_KC), tok(COL_VC),
                  pl.BlockSpec((tm, LORA_W), lambda i: (i, COL_LORA // LORA_W)),
                  const((1, C_W)), const((1, C_W)), const((1, C_W)),
                  const((1, 2 * C_W)), const((1, 2 * C_W)),
                  const((LANES, 2 * C_W)), const((LANES, 2 * C_W)), const((LANES, C_W)),
                  const((C_W, C_W))],
        out_specs=[pl.BlockSpec((tm, C_W), lambda i: (i, 0))] * 9,
        out_shape=[out] * 9,
        compiler_params=_cparams("arbitrary"),
        name="rwkv_prep",
    )(proj, proj, proj, proj, lp['k_k'], lp['k_a'], lp['r_k'], lp['w0'], lp['a0'],
      lp['w_lora'], lp['a_lora'], lp['g_lora'], ones_blockdiag)


K_PAIRS = HEAD_DIM // 2


def _scan_kernel(*refs, tb, mode):
    if mode == "mixed":
        fwd_in, bwd_in = refs[0:6], refs[6:12]
        s0_ref, o_ref, o_mir_ref, sf_ref, s_scr = refs[12:]
        lane = lax.broadcasted_iota(jnp.int32, (1, LANES), 1)
        fwd_lane = (lane % CHAIN_GROUP) < CHAIN_GROUP // 2
    else:
        fwd_in = bwd_in = refs[0:6]
        s0_ref, o_ref, sf_ref, s_scr = refs[6:]
        o_mir_ref = o_ref
    j = pl.program_id(1)

    def pick(idx, t, rows):
        if mode == "fwd":
            return fwd_in[idx][0, t, rows, :]
        if mode == "bwd":
            return bwd_in[idx][0, tb - 1 - t, rows, :]
        return jnp.where(fwd_lane, fwd_in[idx][0, t, rows, :], bwd_in[idx][0, tb - 1 - t, rows, :])

    def row(idx, t, p):
        return jnp.broadcast_to(pick(idx, t, pl.ds(p, 1)), (HEAD_DIM, LANES))

    def fold(x):
        return x + pltpu.roll(x, CHAIN_GROUP, axis=1)

    Z, R, W, K, B, V = range(6)

    @pl.when(j == 0)
    def _():
        s_scr[...] = s0_ref[0]

    u0 = jnp.zeros((HEAD_DIM, LANES), F32)
    for p in range(K_PAIRS):
        u0 = u0 + s_scr[p] * row(Z, 0, p)

    def step(t, u_part):
        u = fold(u_part)
        t_next = jnp.minimum(t + 1, tb - 1)
        vt = pick(V, t, slice(None))
        o = jnp.zeros((HEAD_DIM, LANES), F32)
        u_next = jnp.zeros((HEAD_DIM, LANES), F32)
        for p in range(K_PAIRS):
            s = s_scr[p] * row(W, t, p) + u * row(B, t, p) + vt * row(K, t, p)
            s_scr[p] = s
            o = o + s * row(R, t, p)
            u_next = u_next + s * row(Z, t_next, p)
        o = fold(o)
        if mode != "bwd":
            o_ref[0, t] = o
        if mode != "fwd":
            o_mir_ref[0, tb - 1 - t] = o
        return u_next

    lax.fori_loop(0, tb, step, u0)

    @pl.when(j == pl.num_programs(1) - 1)
    def _():
        sf_ref[0] = s_scr[...]


def _wkv_scan_groups(arrays, s0, t_len, g0, ng, mode):
    tb = 64
    nj = t_len // tb
    nat = lambda g, j: (g0 + g, j, 0, 0)
    mir = lambda g, j: (g0 + g, nj - 1 - j, 0, 0)
    vec = lambda im: pl.BlockSpec((1, tb, K_PAIRS, LANES), im)
    val = lambda im: pl.BlockSpec((1, tb, HEAD_DIM, LANES), im)
    state = pl.BlockSpec((1, K_PAIRS, HEAD_DIM, LANES), lambda g, j: (g0 + g, 0, 0, 0))
    state_out = pl.BlockSpec((1, K_PAIRS, HEAD_DIM, LANES), lambda g, j: (g, 0, 0, 0))
    o_shape = jax.ShapeDtypeStruct((ng, t_len, HEAD_DIM, LANES), F32)
    o_nat = pl.BlockSpec((1, tb, HEAD_DIM, LANES), lambda g, j: (g, j, 0, 0))
    o_mir = pl.BlockSpec((1, tb, HEAD_DIM, LANES), lambda g, j: (g, nj - 1 - j, 0, 0))
    specs = lambda im: [vec(im)] * 5 + [val(im)]
    if mode == "mixed":
        in_specs, operands = specs(nat) + specs(mir) + [state], list(arrays) * 2 + [s0]
        out_specs, out_shape = [o_nat, o_mir, state_out], [o_shape, o_shape]
    else:
        in_specs, operands = specs(nat if mode == "fwd" else mir) + [state], list(arrays) + [s0]
        out_specs, out_shape = [o_nat if mode == "fwd" else o_mir, state_out], [o_shape]
    out_shape = out_shape + [jax.ShapeDtypeStruct((ng, K_PAIRS, HEAD_DIM, LANES), F32)]
    res = pl.pallas_call(
        functools.partial(_scan_kernel, tb=tb, mode=mode),
        grid=(ng, nj),
        in_specs=in_specs,
        out_specs=out_specs,
        out_shape=out_shape,
        scratch_shapes=[pltpu.VMEM((K_PAIRS, HEAD_DIM, LANES), F32)],
        compiler_params=_cparams("arbitrary", "arbitrary"),
        name="wkv_scan_" + mode,
    )(*operands)
    return res[:-1], res[-1]


def _wkv_scan(arrays, s0, t_len):
    groups = s0.shape[0]
    half = groups // 2
    lane_fwd = (np.arange(LANES) % CHAIN_GROUP) < CHAIN_GROUP // 2
    outs, states = [], []
    if half:
        (o,), s = _wkv_scan_groups(arrays, s0, t_len, 0, half, "fwd")
        outs.append(o)
        states.append(s)
    if groups % 2:
        (o_f, o_b), s = _wkv_scan_groups(arrays, s0, t_len, half, 1, "mixed")
        outs.append(jnp.where(lane_fwd, o_f, o_b))
        states.append(s)
    if half:
        (o,), s = _wkv_scan_groups(arrays, s0, t_len, groups - half, half, "bwd")
        outs.append(o)
        states.append(s)
    return jnp.concatenate(outs, axis=0), jnp.concatenate(states, axis=0)


def _rwkv_post_kernel(of_ref, ob_ref, bonus_ref, g_ref, lg_ref, lb_ref, e_ref, o_ref):
    e = e_ref[...]
    o = of_ref[...] + ob_ref[...]
    mu = _segment_sum(o, e) * (1.0 / HEAD_DIM)
    dlt = o - mu
    var = _segment_sum(dlt * dlt, e) * (1.0 / HEAD_DIM)
    y = dlt * lax.rsqrt(var + GN_EPS) * lg_ref[...] + lb_ref[...] + bonus_ref[...]
    o_ref[...] = (y * g_ref[...]).astype(o_ref.dtype)


def _rwkv_post(o_f, o_b, bonus, g, ln_g, ln_b, ones_blockdiag):
    m = o_f.shape[0]
    tm = 256
    tok = pl.BlockSpec((tm, C_W), lambda i: (i, 0))
    const = lambda shape: pl.BlockSpec(shape, lambda i: (0, 0))
    return pl.pallas_call(
        _rwkv_post_kernel,
        grid=(m // tm,),
        in_specs=[tok, tok, tok, tok, const((1, C_W)), const((1, C_W)), const((C_W, C_W))],
        out_specs=tok,
        out_shape=jax.ShapeDtypeStruct((m, C_W), BF16),
        compiler_params=_cparams("arbitrary"),
        name="rwkv_post",
    )(o_f, o_b, bonus, g, ln_g, ln_b, ones_blockdiag)


def _key_vectors_to_chains(x_f, x_b, bsz, t_len):
    x = jnp.stack([x_f.reshape(bsz, t_len, HC, HEAD_DIM), x_b.reshape(bsz, t_len, HC, HEAD_DIM)], 0)
    groups = 2 * bsz * HC // CHAIN_GROUP
    x = x.transpose(2, 4, 0, 1, 3).reshape(t_len, K_PAIRS, 2, groups, CHAIN_GROUP)
    return x.transpose(3, 0, 1, 2, 4).reshape(groups, t_len, K_PAIRS, LANES)


def _values_to_chains(v, bsz, t_len):
    v = v.reshape(bsz, t_len, HC, HEAD_DIM)
    groups = 2 * bsz * HC // CHAIN_GROUP
    x = jnp.stack([v, v], 0).transpose(2, 4, 0, 1, 3).reshape(t_len, HEAD_DIM, groups, CHAIN_GROUP)
    x = x.transpose(2, 0, 1, 3)
    return jnp.concatenate([x, x], axis=-1)


def _values_from_chains(o, bsz, t_len):
    x = o[..., :CHAIN_GROUP].transpose(1, 2, 0, 3).reshape(t_len, HEAD_DIM, 2, bsz, HC)
    x = x.transpose(2, 3, 0, 4, 1).reshape(2, bsz * t_len, C_W)
    return x[0], x[1]


def _states_to_chains(s_f, s_b, bsz):
    x = jnp.stack([s_f, s_b], 0).astype(F32).reshape(2, bsz, HC, HEAD_DIM, K_PAIRS, 2)
    groups = 2 * bsz * HC // CHAIN_GROUP
    x = x.transpose(4, 3, 5, 0, 1, 2).reshape(K_PAIRS, HEAD_DIM, 2, groups, CHAIN_GROUP)
    return x.transpose(3, 0, 1, 2, 4).reshape(groups, K_PAIRS, HEAD_DIM, LANES)


def _states_from_chains(s, bsz):
    groups = s.shape[0]
    x = s.reshape(groups, K_PAIRS, HEAD_DIM, 2, CHAIN_GROUP).transpose(1, 2, 3, 0, 4)
    x = x.reshape(K_PAIRS, HEAD_DIM, 2, 2, bsz, HC).transpose(3, 4, 5, 1, 0, 2)
    x = x.reshape(2, bsz, HC, HEAD_DIM, HEAD_DIM)
    return x[0], x[1]


def _rwkv7(proj, lp, ones_blockdiag, s0_f, s0_b, bsz, t_len):
    z, w_f, kd_f, b_f, w_b, kd_b, b_b, bonus, g = _rwkv_prep(proj, lp, ones_blockdiag)
    r = proj[:, COL_R:COL_R + C_W]
    v = proj[:, COL_VC:COL_VC + C_W]
    to_chains = functools.partial(_key_vectors_to_chains, bsz=bsz, t_len=t_len)
    arrays = (to_chains(z, z), to_chains(r, r), to_chains(w_f, w_b), to_chains(kd_f, kd_b),
              to_chains(b_f, b_b), _values_to_chains(v, bsz, t_len))
    o, s_fin = _wkv_scan(arrays, _states_to_chains(s0_f, s0_b, bsz), t_len)
    o_f, o_b = _values_from_chains(o, bsz, t_len)
    y_c = _rwkv_post(o_f, o_b, bonus, g, lp['ln_g'], lp['ln_b'], ones_blockdiag)
    s_f, s_b = _states_from_chains(s_fin, bsz)
    return y_c, s_f, s_b


def _trunk_layer(x2d, bsz, t_len, m3, row0, row_stride, attend, s0_f, s0_b, lp, ones_blockdiag):
    mod = dict(m3=m3, row0=row0, row_stride=row_stride, bsz=bsz, t_len=t_len)
    h = _norm_mod(x2d, lp['n1'], chunk_shift=0, chunk_scale=1, **mod)
    proj = _matmul(h, lp['w_in'], tm=min(512, x2d.shape[0]), tn=1024)
    y_a = attend(proj)
    y_b = _conv_module(proj, lp['conv_w'], lp['conv_b'], lp['conv_ln_g'], lp['conv_ln_b'], bsz, t_len)
    y_c, s_f, s_b = _rwkv7(proj, lp, ones_blockdiag, s0_f, s0_b, bsz, t_len)
    y = jnp.concatenate([y_a, y_b, y_c], axis=-1)
    x2d = _matmul_gated_residual(y, lp['w_out'], x2d, chunk=2, tm=512, tn=1024, **mod)
    h2 = _norm_mod(x2d, lp['n2'], chunk_shift=3, chunk_scale=4, **mod)
    up = _matmul(h2, lp['w_up'], tm=min(512, x2d.shape[0]), tn=1024)
    act = _ffn_activation(up, lp['ffn_conv_w'], lp['ffn_conv_b'], bsz, t_len)
    x2d = _matmul_gated_residual(act, lp['w_down'], x2d, chunk=5, tm=512, tn=512, **mod)
    return x2d, proj, s_f, s_b


def _permute_w_in(w_in_l):
    sizes = (A_W, A_W, A_W, B_W, B_W, C_W, C_W, C_W, W_RANK, W_RANK, A_RANK, A_RANK, G_RANK)
    starts = np.concatenate([[0], np.cumsum(sizes)])
    part = lambda i: w_in_l[:, starts[i]:starts[i + 1]]
    q, k, v, ga, gb, r, kc, vc, xwf, xwb, xaf, xab, xg = (part(i) for i in range(len(sizes)))
    pad = jnp.zeros((w_in_l.shape[0], PROJ_COLS - int(starts[-1])), w_in_l.dtype)
    return jnp.concatenate([q, k, v, r, kc, vc, ga, gb, xwf, xwb, xaf, xab, xg, pad], axis=1).astype(BF16)


def _padded_lora(lora_fb, rank):
    z = jnp.zeros((rank, C_W), lora_fb.dtype)
    top = jnp.concatenate([lora_fb[0], z], axis=1)
    bot = jnp.concatenate([z, lora_fb[1]], axis=1)
    return jnp.concatenate([top, bot], axis=0).astype(BF16)


def kernel(x_prompt, x_sample, cache_k, cache_v, state_fwd, state_bwd, c, c_ctx, norm1_g, norm2_g, w_mod, b_mod, w_in, w_out, rpb, conv_w, conv_b, conv_ln_g, conv_ln_b, rwkv_w0, rwkv_w_lora, rwkv_a0, rwkv_a_lora, rwkv_g_lora, rwkv_k_k, rwkv_k_a, rwkv_r_k, rwkv_ln_g, rwkv_ln_b, w_up, ffn_conv_w, ffn_conv_b, w_down, final_g):
    bp, tp, d = x_prompt.shape
    bs, ts, _ = x_sample.shape
    depth = w_in.shape[0]
    past = cache_k.shape[2]
    assert bs + 1 <= MOD_ROWS

    cvecs = jnp.concatenate([c_ctx[None, :], c, jnp.zeros((MOD_ROWS - 1 - bs, d), F32)], axis=0)
    m3 = _modulation(cvecs, w_mod, b_mod).reshape(depth * MOD_ROWS, 1, N_MOD * d)

    head_of_lane = np.arange(C_W) // HEAD_DIM
    ones_blockdiag = jnp.asarray(head_of_lane[:, None] == head_of_lane[None, :], BF16)
    cache_k4 = cache_k.reshape(bs, depth, past, A_W)
    cache_v4 = cache_v.reshape(bs, depth, past, A_W)
    zero_state = jnp.zeros((bp, HC, HEAD_DIM, HEAD_DIM), F32)

    xp = x_prompt.reshape(bp * tp, d)
    xs = x_sample.reshape(bs * ts, d)
    ks, vs, sfs, sbs = [], [], [], []
    for l in range(depth):
        row = lambda v: v.reshape(1, -1)
        lp = {
            'n1': norm1_g[l], 'n2': norm2_g[l],
            'w_in': _permute_w_in(w_in[l]), 'w_out': w_out[l].astype(BF16),
            'conv_w': conv_w[l], 'conv_b': conv_b[l], 'conv_ln_g': conv_ln_g[l], 'conv_ln_b': conv_ln_b[l],
            'w0': rwkv_w0[l].reshape(1, 2 * C_W), 'a0': rwkv_a0[l].reshape(1, 2 * C_W),
            'w_lora': _padded_lora(rwkv_w_lora[l], W_RANK), 'a_lora': _padded_lora(rwkv_a_lora[l], A_RANK),
            'g_lora': rwkv_g_lora[l].astype(BF16),
            'k_k': row(rwkv_k_k[l]), 'k_a': row(rwkv_k_a[l]), 'r_k': row(rwkv_r_k[l]),
            'ln_g': row(rwkv_ln_g[l]), 'ln_b': row(rwkv_ln_b[l]),
            'w_up': w_up[l].astype(BF16), 'ffn_conv_w': ffn_conv_w[l], 'ffn_conv_b': ffn_conv_b[l],
            'w_down': w_down[l].astype(BF16),
        }
        ctx_attend = functools.partial(_context_attention, bsz=bp, t_len=tp)
        xp, proj_p, s_f, s_b = _trunk_layer(xp, bp, tp, m3, l * MOD_ROWS, 0, ctx_attend,
                                            zero_state, zero_state, lp, ones_blockdiag)
        ks.append(proj_p[:, COL_K:COL_K + A_W].reshape(bp, tp, HA, HEAD_DIM))
        vs.append(proj_p[:, COL_V:COL_V + A_W].reshape(bp, tp, HA, HEAD_DIM))
        sfs.append(s_f)
        sbs.append(s_b)
        nbr_attend = functools.partial(_neighbourhood_attention, cache_k4=cache_k4, cache_v4=cache_v4,
                                       layer=l, rpb_l=rpb[l], bsz=bs, t_len=ts)
        xs, _, _, _ = _trunk_layer(xs, bs, ts, m3, l * MOD_ROWS + 1, 1, nbr_attend,
                                   state_fwd[:, l], state_bwd[:, l], lp, ones_blockdiag)
    y_prompt = _final_norm(xp, final_g).reshape(bp, tp, d)
    y_sample = _final_norm(xs, final_g).reshape(bs, ts, d)
    return (y_prompt, y_sample, jnp.stack(ks, axis=1), jnp.stack(vs, axis=1),
            jnp.stack(sfs, axis=1), jnp.stack(sbs, axis=1))
```

```python
import functools

import numpy as np
import jax
import jax.numpy as jnp
from jax import lax
from jax.experimental import pallas as pl
from jax.experimental.pallas import tpu as pltpu

F32 = jnp.float32
BF16 = jnp.bfloat16

D_MODEL = 2048
DEPTH = 4
GRID_W = 64
HEAD_DIM = 64
A_W = 768
B_W = 512
C_W = 768
HA = A_W // HEAD_DIM
HC = C_W // HEAD_DIM
WIN_R = 8
WIN_C = 16
CONV_K = 31
W_RANK = 64
A_RANK = 64
G_RANK = 128
D_FF = 5632
N_MOD = 6
ATTN_SCALE = HEAD_DIM ** -0.5
DECAY_SCALE = 0.606531
RMS_EPS = 1e-6
LN_EPS = 1e-5
GN_EPS = 64e-5
L2_EPS = 1e-12
MASK_VALUE = -1e30

COL_Q, COL_K, COL_V = 0, 768, 1536
COL_R, COL_KC, COL_VC = 2304, 3072, 3840
COL_GA, COL_GB = 4608, 5120
COL_LORA = 5632
PROJ_COLS = 6144
LORA_W = 512

MOD_ROWS = 16
LANES = 128
CHAIN_GROUP = 64
VMEM_LIMIT = 56 * 1024 * 1024


def _cparams(*semantics):
    return pltpu.CompilerParams(dimension_semantics=semantics, vmem_limit_bytes=VMEM_LIMIT)


def _sigmoid(x):
    return jax.nn.sigmoid(x)


def _mod_kernel(c_ref, w_ref, b_ref, o_ref):
    c = c_ref[...]
    s = (c * _sigmoid(c)).astype(BF16)
    o_ref[0] = jnp.dot(s, w_ref[0].astype(BF16), preferred_element_type=F32) + b_ref[0]


def _modulation(cvecs, w_mod, b_mod):
    depth, d, n = w_mod.shape
    tn = 1024
    return pl.pallas_call(
        _mod_kernel,
        grid=(depth, n // tn),
        in_specs=[pl.BlockSpec((MOD_ROWS, d), lambda l, j: (0, 0)),
                  pl.BlockSpec((1, d, tn), lambda l, j: (l, 0, j)),
                  pl.BlockSpec((1, 1, tn), lambda l, j: (l, 0, j))],
        out_specs=pl.BlockSpec((1, MOD_ROWS, tn), lambda l, j: (l, 0, j)),
        out_shape=jax.ShapeDtypeStruct((depth, MOD_ROWS, n), F32),
        compiler_params=_cparams("arbitrary", "arbitrary"),
        name="modulation",
    )(cvecs, w_mod, b_mod.reshape(depth, 1, n))


def _norm_mod_kernel(x_ref, g_ref, sc_ref, sh_ref, o_ref):
    x = x_ref[...]
    inv = lax.rsqrt(jnp.mean(x * x, axis=-1, keepdims=True) + RMS_EPS)
    y = (x * inv) * g_ref[...]
    o_ref[...] = (y * (1.0 + sc_ref[0]) + sh_ref[0]).astype(o_ref.dtype)


def _norm_kernel(x_ref, g_ref, o_ref):
    x = x_ref[...]
    inv = lax.rsqrt(jnp.mean(x * x, axis=-1, keepdims=True) + RMS_EPS)
    o_ref[...] = (x * inv) * g_ref[...]


def _mod_spec(width, row0, row_stride, chunk, blocks_per_chunk=1, with_j=False):
    if with_j:
        return pl.BlockSpec((1, 1, width),
                            lambda b, i, j: (row0 + b * row_stride, 0, chunk * blocks_per_chunk + j))
    return pl.BlockSpec((1, 1, width), lambda b, i: (row0 + b * row_stride, 0, chunk))


def _norm_mod(x2d, gain, m3, row0, row_stride, chunk_shift, chunk_scale, bsz, t_len):
    tm = min(t_len, 512)
    nt = t_len // tm
    d = x2d.shape[1]
    return pl.pallas_call(
        _norm_mod_kernel,
        grid=(bsz, nt),
        in_specs=[pl.BlockSpec((tm, d), lambda b, i: (b * nt + i, 0)),
                  pl.BlockSpec((1, d), lambda b, i: (0, 0)),
                  _mod_spec(d, row0, row_stride, chunk_scale),
                  _mod_spec(d, row0, row_stride, chunk_shift)],
        out_specs=pl.BlockSpec((tm, d), lambda b, i: (b * nt + i, 0)),
        out_shape=jax.ShapeDtypeStruct(x2d.shape, BF16),
        compiler_params=_cparams("arbitrary", "arbitrary"),
        name="norm_mod",
    )(x2d, gain.reshape(1, d), m3, m3)


def _final_norm(x2d, gain):
    tm = 512
    m, d = x2d.shape
    return pl.pallas_call(
        _norm_kernel,
        grid=(m // tm,),
        in_specs=[pl.BlockSpec((tm, d), lambda i: (i, 0)),
                  pl.BlockSpec((1, d), lambda i: (0, 0))],
        out_specs=pl.BlockSpec((tm, d), lambda i: (i, 0)),
        out_shape=jax.ShapeDtypeStruct(x2d.shape, F32),
        compiler_params=_cparams("arbitrary"),
        name="final_norm",
    )(x2d, gain.reshape(1, d))


def _mm_kernel(a_ref, b_ref, o_ref):
    o_ref[...] = jnp.dot(a_ref[...], b_ref[...], preferred_element_type=F32).astype(o_ref.dtype)


def _mm_res_kernel(a_ref, b_ref, x_ref, g_ref, o_ref):
    y = jnp.dot(a_ref[...], b_ref[...], preferred_element_type=F32)
    o_ref[...] = x_ref[...] + g_ref[0] * y


def _matmul(a, b, tm, tn, out_dtype=F32):
    m, k = a.shape
    n = b.shape[1]
    return pl.pallas_call(
        _mm_kernel,
        grid=(m // tm, n // tn),
        in_specs=[pl.BlockSpec((tm, k), lambda i, j: (i, 0)),
                  pl.BlockSpec((k, tn), lambda i, j: (0, j))],
        out_specs=pl.BlockSpec((tm, tn), lambda i, j: (i, j)),
        out_shape=jax.ShapeDtypeStruct((m, n), out_dtype),
        compiler_params=_cparams("arbitrary", "arbitrary"),
        name="matmul",
    )(a, b)


def _matmul_gated_residual(a, b, x2d, m3, row0, row_stride, chunk, bsz, t_len, tm, tn):
    m, k = a.shape
    n = b.shape[1]
    tm = min(tm, t_len)
    nt = t_len // tm
    return pl.pallas_call(
        _mm_res_kernel,
        grid=(bsz, nt, n // tn),
        in_specs=[pl.BlockSpec((tm, k), lambda bb, i, j: (bb * nt + i, 0)),
                  pl.BlockSpec((k, tn), lambda bb, i, j: (0, j)),
                  pl.BlockSpec((tm, tn), lambda bb, i, j: (bb * nt + i, j)),
                  _mod_spec(tn, row0, row_stride, chunk, n // tn, with_j=True)],
        out_specs=pl.BlockSpec((tm, tn), lambda bb, i, j: (bb * nt + i, j)),
        out_shape=jax.ShapeDtypeStruct((m, n), F32),
        compiler_params=_cparams("arbitrary", "arbitrary", "arbitrary"),
        name="matmul_gated_residual",
    )(a, b, x2d, m3)


def _dot_nt(a, b):
    return lax.dot_general(a, b, (((1,), (1,)), ((), ())), preferred_element_type=F32)


def _ctx_attn_kernel(q_ref, k_ref, v_ref, o_ref):
    q = q_ref[...]
    k = k_ref[...].astype(BF16)
    v = v_ref[...].astype(BF16)
    lane = lax.broadcasted_iota(jnp.int32, (1, LANES), 1)
    outs = []
    for hh in range(2):
        qh = jnp.where((lane >= hh * HEAD_DIM) & (lane < (hh + 1) * HEAD_DIM), q, 0.0).astype(BF16)
        s = _dot_nt(qh, k) * ATTN_SCALE
        e = jnp.exp(s - jnp.max(s, axis=-1, keepdims=True))
        p = e * (1.0 / jnp.sum(e, axis=-1, keepdims=True))
        outs.append(jnp.dot(p.astype(BF16), v, preferred_element_type=F32))
    o_ref[...] = jnp.where(lane < HEAD_DIM, outs[0], outs[1]).astype(o_ref.dtype)


def _context_attention(proj, bsz, t_len):
    pairs = A_W // LANES
    return pl.pallas_call(
        _ctx_attn_kernel,
        grid=(bsz, pairs),
        in_specs=[pl.BlockSpec((t_len, LANES), lambda b, p: (b, COL_Q // LANES + p)),
                  pl.BlockSpec((t_len, LANES), lambda b, p: (b, COL_K // LANES + p)),
                  pl.BlockSpec((t_len, LANES), lambda b, p: (b, COL_V // LANES + p))],
        out_specs=pl.BlockSpec((t_len, LANES), lambda b, p: (b, p)),
        out_shape=jax.ShapeDtypeStruct((bsz * t_len, A_W), BF16),
        compiler_params=_cparams("arbitrary", "arbitrary"),
        name="context_attention",
    )(proj, proj, proj)


def _nbr_attn_kernel(q_ref, k_ref, v_ref, ck_ref, cv_ref, tbl_ref, o_ref, *, rows):
    i = pl.program_id(1)
    kr = min(WIN_R, rows)
    rs = jnp.clip(i - kr // 2, 0, rows - kr)
    start = pl.multiple_of(rs * GRID_W, GRID_W)
    nloc = kr * GRID_W
    lane = lax.broadcasted_iota(jnp.int32, (1, LANES), 1)
    for p in range(A_W // LANES):
        cols = slice(p * LANES, (p + 1) * LANES)
        qp = q_ref[:, cols]
        kp = k_ref[pl.ds(start, nloc), cols].astype(BF16)
        vp = v_ref[pl.ds(start, nloc), cols].astype(BF16)
        ckp = ck_ref[0, 0, :, cols].astype(BF16)
        cvp = cv_ref[0, 0, :, cols].astype(BF16)
        outs = []
        for hh in range(2):
            qh = jnp.where((lane >= hh * HEAD_DIM) & (lane < (hh + 1) * HEAD_DIM), qp, 0.0).astype(BF16)
            tbl = tbl_ref[0, 2 * p + hh]
            s_loc = jnp.where(tbl > 0.5 * MASK_VALUE, _dot_nt(qh, kp) * ATTN_SCALE + tbl, MASK_VALUE)
            s_ctx = _dot_nt(qh, ckp) * ATTN_SCALE
            mx = jnp.maximum(jnp.max(s_loc, axis=-1, keepdims=True), jnp.max(s_ctx, axis=-1, keepdims=True))
            e_loc = jnp.exp(s_loc - mx)
            e_ctx = jnp.exp(s_ctx - mx)
            inv = 1.0 / (jnp.sum(e_loc, axis=-1, keepdims=True) + jnp.sum(e_ctx, axis=-1, keepdims=True))
            outs.append(jnp.dot((e_loc * inv).astype(BF16), vp, preferred_element_type=F32)
                        + jnp.dot((e_ctx * inv).astype(BF16), cvp, preferred_element_type=F32))
        o_ref[:, cols] = jnp.where(lane < HEAD_DIM, outs[0], outs[1]).astype(o_ref.dtype)


def _bias_tables(rpb_l, rows):
    kr = min(WIN_R, rows)
    offs = sorted({i - int(np.clip(i - kr // 2, 0, rows - kr)) for i in range(rows)})
    qcol = np.arange(GRID_W)
    kcol = np.arange(GRID_W)
    win_start = np.clip(qcol - WIN_C // 2, 0, GRID_W - WIN_C)
    valid = (kcol[None, :] >= win_start[:, None]) & (kcol[None, :] < win_start[:, None] + WIN_C)
    col_idx = np.clip(kcol[None, :] - qcol[:, None] + WIN_C - 1, 0, 2 * WIN_C - 2)
    row_idx = np.stack([np.arange(kr) - off + WIN_R - 1 for off in offs])
    flat = row_idx[:, None, :, None] * (2 * WIN_C - 1) + col_idx[None, :, None, :]
    heads = rpb_l.shape[0]
    b = jnp.take(rpb_l.astype(F32).reshape(heads, -1), flat.reshape(-1), axis=1)
    b = b.reshape(heads, len(offs), GRID_W, kr, GRID_W)
    b = jnp.where(valid[None, None, :, None, :], b, MASK_VALUE)
    return b.transpose(1, 0, 2, 3, 4).reshape(len(offs), heads, GRID_W, kr * GRID_W), offs


def _neighbourhood_attention(proj, cache_k4, cache_v4, layer, rpb_l, bsz, t_len):
    rows = t_len // GRID_W
    kr = min(WIN_R, rows)
    tables, offs = _bias_tables(rpb_l, rows)
    assert offs == list(range(len(offs)))
    past = cache_k4.shape[2]

    def variant(i):
        return i - jnp.clip(i - kr // 2, 0, rows - kr)

    return pl.pallas_call(
        functools.partial(_nbr_attn_kernel, rows=rows),
        grid=(bsz, rows),
        in_specs=[pl.BlockSpec((GRID_W, A_W), lambda b, i: (b * rows + i, COL_Q // A_W)),
                  pl.BlockSpec((t_len, A_W), lambda b, i: (b, COL_K // A_W)),
                  pl.BlockSpec((t_len, A_W), lambda b, i: (b, COL_V // A_W)),
                  pl.BlockSpec((1, 1, past, A_W), lambda b, i: (b, layer, 0, 0)),
                  pl.BlockSpec((1, 1, past, A_W), lambda b, i: (b, layer, 0, 0)),
                  pl.BlockSpec((1, HA, GRID_W, kr * GRID_W), lambda b, i: (variant(i), 0, 0, 0))],
        out_specs=pl.BlockSpec((GRID_W, A_W), lambda b, i: (b * rows + i, 0)),
        out_shape=jax.ShapeDtypeStruct((bsz * t_len, A_W), BF16),
        compiler_params=_cparams("arbitrary", "arbitrary"),
        name="neighbourhood_attention",
    )(proj, proj, proj, cache_k4, cache_v4, tables)


CONV_CHUNK = 64
CONV_PAD = 16


def _fill_padded(pad_ref, values, t_len):
    width = pad_ref.shape[1]
    pad_ref[0:CONV_PAD, :] = jnp.zeros((CONV_PAD, width), F32)
    pad_ref[CONV_PAD + t_len:2 * CONV_PAD + t_len, :] = jnp.zeros((CONV_PAD, width), F32)
    pad_ref[CONV_PAD:CONV_PAD + t_len, :] = values


def _conv_taps(pad_ref, w_ref, t0, taps):
    left = (taps - 1) // 2
    window = pad_ref[pl.ds(t0, CONV_CHUNK + 2 * CONV_PAD), :]
    acc = jnp.zeros((CONV_CHUNK, pad_ref.shape[1]), F32)
    for kk in range(taps):
        off = CONV_PAD - left + kk
        acc = acc + window[off:off + CONV_CHUNK, :] * w_ref[kk:kk + 1, :]
    return acc


def _conv_module_kernel(a_ref, b_ref, w_ref, cb_ref, g_ref, lb_ref, o_ref, pad_ref):
    t_len = a_ref.shape[0]
    _fill_padded(pad_ref, a_ref[...] * _sigmoid(b_ref[...]), t_len)

    def chunk(c, carry):
        t0 = pl.multiple_of(c * CONV_CHUNK, CONV_CHUNK)
        u = _conv_taps(pad_ref, w_ref, t0, CONV_K) + cb_ref[...]
        mu = jnp.mean(u, axis=-1, keepdims=True)
        dlt = u - mu
        var = jnp.mean(dlt * dlt, axis=-1, keepdims=True)
        y = dlt * lax.rsqrt(var + LN_EPS) * g_ref[...] + lb_ref[...]
        o_ref[pl.ds(t0, CONV_CHUNK), :] = (y * _sigmoid(y)).astype(o_ref.dtype)
        return carry

    lax.fori_loop(0, t_len // CONV_CHUNK, chunk, 0)


def _conv_module(proj, conv_w, conv_b, ln_g, ln_b, bsz, t_len):
    row = lambda v: v.reshape(1, B_W)
    return pl.pallas_call(
        _conv_module_kernel,
        grid=(bsz,),
        in_specs=[pl.BlockSpec((t_len, B_W), lambda b: (b, COL_GA // B_W)),
                  pl.BlockSpec((t_len, B_W), lambda b: (b, COL_GB // B_W)),
                  pl.BlockSpec((CONV_K, B_W), lambda b: (0, 0)),
                  pl.BlockSpec((1, B_W), lambda b: (0, 0)),
                  pl.BlockSpec((1, B_W), lambda b: (0, 0)),
                  pl.BlockSpec((1, B_W), lambda b: (0, 0))],
        out_specs=pl.BlockSpec((t_len, B_W), lambda b: (b, 0)),
        out_shape=jax.ShapeDtypeStruct((bsz * t_len, B_W), BF16),
        scratch_shapes=[pltpu.VMEM((t_len + 2 * CONV_PAD, B_W), F32)],
        compiler_params=_cparams("arbitrary"),
        name="conv_module",
    )(proj, proj, conv_w, row(conv_b), row(ln_g), row(ln_b))


FFN_ROWS = 2048
FFN_COLS = 256


def _ffn_up_kernel(a_ref, wu_ref, wg_ref, cw_ref, cb_ref, o_ref, *, t_len):
    a = a_ref[...]
    u = jnp.dot(a, wu_ref[...], preferred_element_type=F32)
    g = jnp.dot(a, wg_ref[...], preferred_element_type=F32)
    rows = u.shape[0]
    pos = lax.broadcasted_iota(jnp.int32, (rows, 1), 0) % t_len
    prev = jnp.where(pos == 0, 0.0, pltpu.roll(u, 1, axis=0))
    nxt = jnp.where(pos == t_len - 1, 0.0, pltpu.roll(u, rows - 1, axis=0))
    y = prev * cw_ref[0:1, :] + u * cw_ref[1:2, :] + nxt * cw_ref[2:3, :] + cb_ref[...]
    o_ref[...] = (y * _sigmoid(y) * g).astype(o_ref.dtype)


def _ffn_up(h2, w_up, dw_w, dw_b, t_len):
    m, k = h2.shape
    assert FFN_ROWS % t_len == 0 and m % FFN_ROWS == 0
    nc = D_FF // FFN_COLS
    return pl.pallas_call(
        functools.partial(_ffn_up_kernel, t_len=t_len),
        grid=(m // FFN_ROWS, nc),
        in_specs=[pl.BlockSpec((FFN_ROWS, k), lambda i, j: (i, 0)),
                  pl.BlockSpec((k, FFN_COLS), lambda i, j: (0, j)),
                  pl.BlockSpec((k, FFN_COLS), lambda i, j: (0, nc + j)),
                  pl.BlockSpec((3, FFN_COLS), lambda i, j: (0, j)),
                  pl.BlockSpec((1, FFN_COLS), lambda i, j: (0, j))],
        out_specs=pl.BlockSpec((FFN_ROWS, FFN_COLS), lambda i, j: (i, j)),
        out_shape=jax.ShapeDtypeStruct((m, D_FF), BF16),
        compiler_params=_cparams("arbitrary", "arbitrary"),
        name="ffn_up",
    )(h2, w_up, w_up, dw_w, dw_b.reshape(1, D_FF))


def _segment_sum(x, ones_blockdiag):
    hi = x.astype(BF16)
    lo = (x - hi.astype(F32)).astype(BF16)
    return (jnp.dot(hi, ones_blockdiag, preferred_element_type=F32)
            + jnp.dot(lo, ones_blockdiag, preferred_element_type=F32))


def _rwkv_prep_kernel(r_ref, k_ref, v_ref, s_ref, kkw_ref, ka_ref, rk_ref, w0_ref, a0_ref, wl_ref, al_ref,
                      gl_ref, e_ref, z_o, wf_o, kdf_o, bf_o, wb_o, kdb_o, bb_o, bonus_o, g_o):
    r = r_ref[...]
    k = k_ref[...]
    e = e_ref[...]
    kk = k * kkw_ref[...]
    kk = kk * lax.rsqrt(_segment_sum(kk * kk, e) + L2_EPS)
    xs = s_ref[...]
    xw = jnp.tanh(xs[:, 0:LANES]).astype(BF16)
    xa = xs[:, LANES:2 * LANES].astype(BF16)
    xg = _sigmoid(xs[:, 2 * LANES:3 * LANES]).astype(BF16)
    w_all = jnp.exp(-DECAY_SCALE * _sigmoid(w0_ref[...] + jnp.dot(xw, wl_ref[...], preferred_element_type=F32)))
    a_all = _sigmoid(a0_ref[...] + jnp.dot(xa, al_ref[...], preferred_element_type=F32))
    g_o[...] = jnp.dot(xg, gl_ref[...], preferred_element_type=F32)
    z_o[...] = -kk
    ka = ka_ref[...]
    kd_sum = None
    for d, (w_o, kd_o, b_o) in enumerate(((wf_o, kdf_o, bf_o), (wb_o, kdb_o, bb_o))):
        a = a_all[:, d * C_W:(d + 1) * C_W]
        kd = k * (1.0 + (a - 1.0) * ka)
        w_o[...] = w_all[:, d * C_W:(d + 1) * C_W]
        kd_o[...] = kd
        b_o[...] = kk * a
        kd_sum = kd if kd_sum is None else kd_sum + kd
    bonus_o[...] = _segment_sum(r * kd_sum * rk_ref[...], e) * v_ref[...]


def _rwkv_prep(proj, lp, ones_blockdiag):
    m = proj.shape[0]
    tm = 256
    tok = lambda col: pl.BlockSpec((tm, C_W), lambda i, c=col // C_W: (i, c))
    const = lambda shape: pl.BlockSpec(shape, lambda i: (0, 0))
    out = jax.ShapeDtypeStruct((m, C_W), F32)
    return pl.pallas_call(
        _rwkv_prep_kernel,
        grid=(m // tm,),
        in_specs=[tok(COL_R), tok(COL_KC), tok(COL_VC),
                  pl.BlockSpec((tm, LORA_W), lambda i: (i, COL_LORA // LORA_W)),
                  const((1, C_W)), const((1, C_W)), const((1, C_W)),
                  const((1, 2 * C_W)), const((1, 2 * C_W)),
                  const((LANES, 2 * C_W)), const((LANES, 2 * C_W)), const((LANES, C_W)),
                  const((C_W, C_W))],
        out_specs=[pl.BlockSpec((tm, C_W), lambda i: (i, 0))] * 9,
        out_shape=[out] * 9,
        compiler_params=_cparams("arbitrary"),
        name="rwkv_prep",
    )(proj, proj, proj, proj, lp['k_k'], lp['k_a'], lp['r_k'], lp['w0'], lp['a0'],
      lp['w_lora'], lp['a_lora'], lp['g_lora'], ones_blockdiag)


V_SUB = HEAD_DIM // 2


def _scan_kernel(*refs, tb, mode):
    if mode == "mixed":
        fwd_in, bwd_in = refs[0:6], refs[6:12]
        s0_ref, o_ref, o_mir_ref, sf_ref, s_scr = refs[12:]
        lane = lax.broadcasted_iota(jnp.int32, (1, LANES), 1)
        fwd_lane = (lane % CHAIN_GROUP) < CHAIN_GROUP // 2
    else:
        fwd_in = bwd_in = refs[0:6]
        s0_ref, o_ref, sf_ref, s_scr = refs[6:]
        o_mir_ref = o_ref
    j = pl.program_id(1)

    def pick(idx, t, rows):
        if mode == "fwd":
            return fwd_in[idx][0, t, rows, :]
        if mode == "bwd":
            return bwd_in[idx][0, tb - 1 - t, rows, :]
        return jnp.where(fwd_lane, fwd_in[idx][0, t, rows, :], bwd_in[idx][0, tb - 1 - t, rows, :])

    def row(idx, t, p):
        return jnp.broadcast_to(pick(idx, t, pl.ds(p, 1)), (V_SUB, LANES))

    Z, R, W, K, B, V = range(6)

    @pl.when(j == 0)
    def _():
        s_scr[...] = s0_ref[0]

    u0 = jnp.zeros((V_SUB, LANES), F32)
    for p in range(HEAD_DIM):
        u0 = u0 + s_scr[p] * row(Z, 0, p)

    def step(t, u):
        t_next = jnp.minimum(t + 1, tb - 1)
        vt = pick(V, t, slice(None))
        o = jnp.zeros((V_SUB, LANES), F32)
        u_next = jnp.zeros((V_SUB, LANES), F32)
        for p in range(HEAD_DIM):
            s = s_scr[p] * row(W, t, p) + u * row(B, t, p) + vt * row(K, t, p)
            s_scr[p] = s
            o = o + s * row(R, t, p)
            u_next = u_next + s * row(Z, t_next, p)
        if mode != "bwd":
            o_ref[0, t] = o
        if mode != "fwd":
            o_mir_ref[0, tb - 1 - t] = o
        return u_next

    lax.fori_loop(0, tb, step, u0)

    @pl.when(j == pl.num_programs(1) - 1)
    def _():
        sf_ref[0] = s_scr[...]


def _wkv_scan_groups(arrays, s0, t_len, g0, ng, mode):
    tb = 32
    nj = t_len // tb
    nat = lambda g, j: (g0 + g, j, 0, 0)
    mir = lambda g, j: (g0 + g, nj - 1 - j, 0, 0)
    vec = lambda im: pl.BlockSpec((1, tb, HEAD_DIM, LANES), im)
    val = lambda im: pl.BlockSpec((1, tb, V_SUB, LANES), im)
    state = pl.BlockSpec((1, HEAD_DIM, V_SUB, LANES), lambda g, j: (g0 + g, 0, 0, 0))
    state_out = pl.BlockSpec((1, HEAD_DIM, V_SUB, LANES), lambda g, j: (g, 0, 0, 0))
    o_shape = jax.ShapeDtypeStruct((ng, t_len, V_SUB, LANES), F32)
    o_nat = pl.BlockSpec((1, tb, V_SUB, LANES), lambda g, j: (g, j, 0, 0))
    o_mir = pl.BlockSpec((1, tb, V_SUB, LANES), lambda g, j: (g, nj - 1 - j, 0, 0))
    specs = lambda im: [vec(im)] * 5 + [val(im)]
    if mode == "mixed":
        in_specs, operands = specs(nat) + specs(mir) + [state], list(arrays) * 2 + [s0]
        out_specs, out_shape = [o_nat, o_mir, state_out], [o_shape, o_shape]
    else:
        in_specs, operands = specs(nat if mode == "fwd" else mir) + [state], list(arrays) + [s0]
        out_specs, out_shape = [o_nat if mode == "fwd" else o_mir, state_out], [o_shape]
    out_shape = out_shape + [jax.ShapeDtypeStruct((ng, HEAD_DIM, V_SUB, LANES), F32)]
    res = pl.pallas_call(
        functools.partial(_scan_kernel, tb=tb, mode=mode),
        grid=(ng, nj),
        in_specs=in_specs,
        out_specs=out_specs,
        out_shape=out_shape,
        scratch_shapes=[pltpu.VMEM((HEAD_DIM, V_SUB, LANES), F32)],
        compiler_params=_cparams("arbitrary", "arbitrary"),
        name="wkv_scan_" + mode,
    )(*operands)
    return res[:-1], res[-1]


def _wkv_scan(arrays, s0, t_len):
    groups = s0.shape[0]
    half = groups // 2
    lane_fwd = (np.arange(LANES) % CHAIN_GROUP) < CHAIN_GROUP // 2
    outs, states = [], []
    if half:
        (o,), s = _wkv_scan_groups(arrays, s0, t_len, 0, half, "fwd")
        outs.append(o)
        states.append(s)
    if groups % 2:
        (o_f, o_b), s = _wkv_scan_groups(arrays, s0, t_len, half, 1, "mixed")
        outs.append(jnp.where(lane_fwd, o_f, o_b))
        states.append(s)
    if half:
        (o,), s = _wkv_scan_groups(arrays, s0, t_len, groups - half, half, "bwd")
        outs.append(o)
        states.append(s)
    return jnp.concatenate(outs, axis=0), jnp.concatenate(states, axis=0)


def _rwkv_post_kernel(of_ref, ob_ref, bonus_ref, g_ref, lg_ref, lb_ref, e_ref, o_ref):
    e = e_ref[...]
    o = of_ref[...] + ob_ref[...]
    mu = _segment_sum(o, e) * (1.0 / HEAD_DIM)
    dlt = o - mu
    var = _segment_sum(dlt * dlt, e) * (1.0 / HEAD_DIM)
    y = dlt * lax.rsqrt(var + GN_EPS) * lg_ref[...] + lb_ref[...] + bonus_ref[...]
    o_ref[...] = (y * g_ref[...]).astype(o_ref.dtype)


def _rwkv_post(o_f, o_b, bonus, g, ln_g, ln_b, ones_blockdiag):
    m = o_f.shape[0]
    tm = 256
    tok = pl.BlockSpec((tm, C_W), lambda i: (i, 0))
    const = lambda shape: pl.BlockSpec(shape, lambda i: (0, 0))
    return pl.pallas_call(
        _rwkv_post_kernel,
        grid=(m // tm,),
        in_specs=[tok, tok, tok, tok, const((1, C_W)), const((1, C_W)), const((C_W, C_W))],
        out_specs=tok,
        out_shape=jax.ShapeDtypeStruct((m, C_W), BF16),
        compiler_params=_cparams("arbitrary"),
        name="rwkv_post",
    )(o_f, o_b, bonus, g, ln_g, ln_b, ones_blockdiag)


def _key_vectors_to_chains(x_f, x_b, bsz, t_len):
    x = jnp.stack([x_f.reshape(bsz, t_len, HC, HEAD_DIM), x_b.reshape(bsz, t_len, HC, HEAD_DIM)], 0)
    groups = 2 * bsz * HC // CHAIN_GROUP
    x = x.transpose(2, 4, 0, 1, 3).reshape(t_len, HEAD_DIM, groups, CHAIN_GROUP).transpose(2, 0, 1, 3)
    return jnp.concatenate([x, x], axis=-1)


def _values_to_chains(v, bsz, t_len):
    v = v.reshape(bsz, t_len, HC, 2, V_SUB)
    x = jnp.stack([v, v], 0)
    groups = 2 * bsz * HC // CHAIN_GROUP
    x = x.transpose(2, 5, 4, 0, 1, 3).reshape(t_len, V_SUB, 2, groups, CHAIN_GROUP)
    return x.transpose(3, 0, 1, 2, 4).reshape(groups, t_len, V_SUB, LANES)


def _values_from_chains(o, bsz, t_len):
    groups = o.shape[0]
    x = o.reshape(groups, t_len, V_SUB, 2, CHAIN_GROUP).transpose(1, 2, 3, 0, 4)
    x = x.reshape(t_len, V_SUB, 2, 2, bsz, HC).transpose(3, 4, 0, 5, 2, 1)
    x = x.reshape(2, bsz * t_len, C_W)
    return x[0], x[1]


def _states_to_chains(s_f, s_b, bsz):
    x = jnp.stack([s_f, s_b], 0).astype(F32).reshape(2, bsz, HC, 2, V_SUB, HEAD_DIM)
    groups = 2 * bsz * HC // CHAIN_GROUP
    x = x.transpose(5, 4, 3, 0, 1, 2).reshape(HEAD_DIM, V_SUB, 2, groups, CHAIN_GROUP)
    return x.transpose(3, 0, 1, 2, 4).reshape(groups, HEAD_DIM, V_SUB, LANES)


def _states_from_chains(s, bsz):
    groups = s.shape[0]
    x = s.reshape(groups, HEAD_DIM, V_SUB, 2, CHAIN_GROUP).transpose(1, 2, 3, 0, 4)
    x = x.reshape(HEAD_DIM, V_SUB, 2, 2, bsz, HC).transpose(3, 4, 5, 2, 1, 0)
    x = x.reshape(2, bsz, HC, HEAD_DIM, HEAD_DIM)
    return x[0], x[1]


def _rwkv7(proj, lp, ones_blockdiag, s0_f, s0_b, bsz, t_len):
    z, w_f, kd_f, b_f, w_b, kd_b, b_b, bonus, g = _rwkv_prep(proj, lp, ones_blockdiag)
    r = proj[:, COL_R:COL_R + C_W]
    v = proj[:, COL_VC:COL_VC + C_W]
    to_chains = functools.partial(_key_vectors_to_chains, bsz=bsz, t_len=t_len)
    arrays = (to_chains(z, z), to_chains(r, r), to_chains(w_f, w_b), to_chains(kd_f, kd_b),
              to_chains(b_f, b_b), _values_to_chains(v, bsz, t_len))
    o, s_fin = _wkv_scan(arrays, _states_to_chains(s0_f, s0_b, bsz), t_len)
    o_f, o_b = _values_from_chains(o, bsz, t_len)
    y_c = _rwkv_post(o_f, o_b, bonus, g, lp['ln_g'], lp['ln_b'], ones_blockdiag)
    s_f, s_b = _states_from_chains(s_fin, bsz)
    return y_c, s_f, s_b


def _trunk_layer(x2d, bsz, t_len, m3, row0, row_stride, attend, s0_f, s0_b, lp, ones_blockdiag):
    mod = dict(m3=m3, row0=row0, row_stride=row_stride, bsz=bsz, t_len=t_len)
    h = _norm_mod(x2d, lp['n1'], chunk_shift=0, chunk_scale=1, **mod)
    proj = _matmul(h, lp['w_in'], tm=min(512, x2d.shape[0]), tn=1024)
    y_a = attend(proj)
    y_b = _conv_module(proj, lp['conv_w'], lp['conv_b'], lp['conv_ln_g'], lp['conv_ln_b'], bsz, t_len)
    y_c, s_f, s_b = _rwkv7(proj, lp, ones_blockdiag, s0_f, s0_b, bsz, t_len)
    y = jnp.concatenate([y_a, y_b, y_c], axis=-1)
    x2d = _matmul_gated_residual(y, lp['w_out'], x2d, chunk=2, tm=512, tn=1024, **mod)
    h2 = _norm_mod(x2d, lp['n2'], chunk_shift=3, chunk_scale=4, **mod)
    act = _ffn_up(h2, lp['w_up'], lp['ffn_conv_w'], lp['ffn_conv_b'], t_len)
    x2d = _matmul_gated_residual(act, lp['w_down'], x2d, chunk=5, tm=512, tn=512, **mod)
    return x2d, proj, s_f, s_b


def _permute_w_in(w_in_l):
    sizes = (A_W, A_W, A_W, B_W, B_W, C_W, C_W, C_W, W_RANK, W_RANK, A_RANK, A_RANK, G_RANK)
    starts = np.concatenate([[0], np.cumsum(sizes)])
    part = lambda i: w_in_l[:, starts[i]:starts[i + 1]]
    q, k, v, ga, gb, r, kc, vc, xwf, xwb, xaf, xab, xg = (part(i) for i in range(len(sizes)))
    pad = jnp.zeros((w_in_l.shape[0], PROJ_COLS - int(starts[-1])), w_in_l.dtype)
    return jnp.concatenate([q, k, v, r, kc, vc, ga, gb, xwf, xwb, xaf, xab, xg, pad], axis=1).astype(BF16)


def _padded_lora(lora_fb, rank):
    z = jnp.zeros((rank, C_W), lora_fb.dtype)
    top = jnp.concatenate([lora_fb[0], z], axis=1)
    bot = jnp.concatenate([z, lora_fb[1]], axis=1)
    return jnp.concatenate([top, bot], axis=0).astype(BF16)


def kernel(x_prompt, x_sample, cache_k, cache_v, state_fwd, state_bwd, c, c_ctx, norm1_g, norm2_g, w_mod, b_mod, w_in, w_out, rpb, conv_w, conv_b, conv_ln_g, conv_ln_b, rwkv_w0, rwkv_w_lora, rwkv_a0, rwkv_a_lora, rwkv_g_lora, rwkv_k_k, rwkv_k_a, rwkv_r_k, rwkv_ln_g, rwkv_ln_b, w_up, ffn_conv_w, ffn_conv_b, w_down, final_g):
    bp, tp, d = x_prompt.shape
    bs, ts, _ = x_sample.shape
    depth = w_in.shape[0]
    past = cache_k.shape[2]
    assert bs + 1 <= MOD_ROWS

    cvecs = jnp.concatenate([c_ctx[None, :], c, jnp.zeros((MOD_ROWS - 1 - bs, d), F32)], axis=0)
    m3 = _modulation(cvecs, w_mod, b_mod).reshape(depth * MOD_ROWS, 1, N_MOD * d)

    head_of_lane = np.arange(C_W) // HEAD_DIM
    ones_blockdiag = jnp.asarray(head_of_lane[:, None] == head_of_lane[None, :], BF16)
    cache_k4 = cache_k.reshape(bs, depth, past, A_W)
    cache_v4 = cache_v.reshape(bs, depth, past, A_W)
    zero_state = jnp.zeros((bp, HC, HEAD_DIM, HEAD_DIM), F32)

    xp = x_prompt.reshape(bp * tp, d)
    xs = x_sample.reshape(bs * ts, d)
    ks, vs, sfs, sbs = [], [], [], []
    for l in range(depth):
        row = lambda v: v.reshape(1, -1)
        lp = {
            'n1': norm1_g[l], 'n2': norm2_g[l],
            'w_in': _permute_w_in(w_in[l]), 'w_out': w_out[l].astype(BF16),
            'conv_w': conv_w[l], 'conv_b': conv_b[l], 'conv_ln_g': conv_ln_g[l], 'conv_ln_b': conv_ln_b[l],
            'w0': rwkv_w0[l].reshape(1, 2 * C_W), 'a0': rwkv_a0[l].reshape(1, 2 * C_W),
            'w_lora': _padded_lora(rwkv_w_lora[l], W_RANK), 'a_lora': _padded_lora(rwkv_a_lora[l], A_RANK),
            'g_lora': rwkv_g_lora[l].astype(BF16),
            'k_k': row(rwkv_k_k[l]), 'k_a': row(rwkv_k_a[l]), 'r_k': row(rwkv_r_k[l]),
            'ln_g': row(rwkv_ln_g[l]), 'ln_b': row(rwkv_ln_b[l]),
            'w_up': w_up[l].astype(BF16), 'ffn_conv_w': ffn_conv_w[l], 'ffn_conv_b': ffn_conv_b[l],
            'w_down': w_down[l].astype(BF16),
        }
        ctx_attend = functools.partial(_context_attention, bsz=bp, t_len=tp)
        xp, proj_p, s_f, s_b = _trunk_layer(xp, bp, tp, m3, l * MOD_ROWS, 0, ctx_attend,
                                            zero_state, zero_state, lp, ones_blockdiag)
        ks.append(proj_p[:, COL_K:COL_K + A_W].reshape(bp, tp, HA, HEAD_DIM))
        vs.append(proj_p[:, COL_V:COL_V + A_W].reshape(bp, tp, HA, HEAD_DIM))
        sfs.append(s_f)
        sbs.append(s_b)
        nbr_attend = functools.partial(_neighbourhood_attention, cache_k4=cache_k4, cache_v4=cache_v4,
                                       layer=l, rpb_l=rpb[l], bsz=bs, t_len=ts)
        xs, _, _, _ = _trunk_layer(xs, bs, ts, m3, l * MOD_ROWS + 1, 1, nbr_attend,
                                   state_fwd[:, l], state_bwd[:, l], lp, ones_blockdiag)
    y_prompt = _final_norm(xp, final_g).reshape(bp, tp, d)
    y_sample = _final_norm(xs, final_g).reshape(bs, ts, d)
    return (y_prompt, y_sample, jnp.stack(ks, axis=1), jnp.stack(vs, axis=1),
            jnp.stack(sfs, axis=1), jnp.stack(sbs, axis=1))
```

```python
import functools

import numpy as np
import jax
import jax.numpy as jnp
from jax import lax
from jax.experimental import pallas as pl
from jax.experimental.pallas import tpu as pltpu

F32 = jnp.float32
BF16 = jnp.bfloat16

D_MODEL = 2048
DEPTH = 4
GRID_W = 64
HEAD_DIM = 64
A_W = 768
B_W = 512
C_W = 768
HA = A_W // HEAD_DIM
HC = C_W // HEAD_DIM
WIN_R = 8
WIN_C = 16
CONV_K = 31
W_RANK = 64
A_RANK = 64
G_RANK = 128
D_FF = 5632
N_MOD = 6
ATTN_SCALE = HEAD_DIM ** -0.5
DECAY_SCALE = 0.606531
RMS_EPS = 1e-6
LN_EPS = 1e-5
GN_EPS = 64e-5
L2_EPS = 1e-12
MASK_VALUE = -1e30

COL_Q, COL_K, COL_V = 0, 768, 1536
COL_R, COL_KC, COL_VC = 2304, 3072, 3840
COL_GA, COL_GB = 4608, 5120
COL_LORA = 5632
PROJ_COLS = 6144
LORA_W = 512

MOD_ROWS = 16
LANES = 128
CHAIN_GROUP = 64
VMEM_LIMIT = 56 * 1024 * 1024


def _cparams(*semantics):
    return pltpu.CompilerParams(dimension_semantics=semantics, vmem_limit_bytes=VMEM_LIMIT)


def _sigmoid(x):
    return jax.nn.sigmoid(x)


def _mod_kernel(c_ref, w_ref, b_ref, o_ref):
    c = c_ref[...]
    s = (c * _sigmoid(c)).astype(BF16)
    o_ref[0] = jnp.dot(s, w_ref[0].astype(BF16), preferred_element_type=F32) + b_ref[0]


def _modulation(cvecs, w_mod, b_mod):
    depth, d, n = w_mod.shape
    tn = 1024
    return pl.pallas_call(
        _mod_kernel,
        grid=(depth, n // tn),
        in_specs=[pl.BlockSpec((MOD_ROWS, d), lambda l, j: (0, 0)),
                  pl.BlockSpec((1, d, tn), lambda l, j: (l, 0, j)),
                  pl.BlockSpec((1, 1, tn), lambda l, j: (l, 0, j))],
        out_specs=pl.BlockSpec((1, MOD_ROWS, tn), lambda l, j: (l, 0, j)),
        out_shape=jax.ShapeDtypeStruct((depth, MOD_ROWS, n), F32),
        compiler_params=_cparams("arbitrary", "arbitrary"),
        name="modulation",
    )(cvecs, w_mod, b_mod.reshape(depth, 1, n))


def _norm_mod_kernel(x_ref, g_ref, sc_ref, sh_ref, o_ref):
    x = x_ref[...]
    inv = lax.rsqrt(jnp.mean(x * x, axis=-1, keepdims=True) + RMS_EPS)
    y = (x * inv) * g_ref[...]
    o_ref[...] = (y * (1.0 + sc_ref[0]) + sh_ref[0]).astype(o_ref.dtype)


def _norm_kernel(x_ref, g_ref, o_ref):
    x = x_ref[...]
    inv = lax.rsqrt(jnp.mean(x * x, axis=-1, keepdims=True) + RMS_EPS)
    o_ref[...] = (x * inv) * g_ref[...]


def _mod_spec(width, row0, row_stride, chunk, blocks_per_chunk=1, with_j=False):
    if with_j:
        return pl.BlockSpec((1, 1, width),
                            lambda b, i, j: (row0 + b * row_stride, 0, chunk * blocks_per_chunk + j))
    return pl.BlockSpec((1, 1, width), lambda b, i: (row0 + b * row_stride, 0, chunk))


def _norm_mod(x2d, gain, m3, row0, row_stride, chunk_shift, chunk_scale, bsz, t_len):
    tm = min(t_len, 512)
    nt = t_len // tm
    d = x2d.shape[1]
    return pl.pallas_call(
        _norm_mod_kernel,
        grid=(bsz, nt),
        in_specs=[pl.BlockSpec((tm, d), lambda b, i: (b * nt + i, 0)),
                  pl.BlockSpec((1, d), lambda b, i: (0, 0)),
                  _mod_spec(d, row0, row_stride, chunk_scale),
                  _mod_spec(d, row0, row_stride, chunk_shift)],
        out_specs=pl.BlockSpec((tm, d), lambda b, i: (b * nt + i, 0)),
        out_shape=jax.ShapeDtypeStruct(x2d.shape, BF16),
        compiler_params=_cparams("arbitrary", "arbitrary"),
        name="norm_mod",
    )(x2d, gain.reshape(1, d), m3, m3)


def _final_norm(x2d, gain):
    tm = 512
    m, d = x2d.shape
    return pl.pallas_call(
        _norm_kernel,
        grid=(m // tm,),
        in_specs=[pl.BlockSpec((tm, d), lambda i: (i, 0)),
                  pl.BlockSpec((1, d), lambda i: (0, 0))],
        out_specs=pl.BlockSpec((tm, d), lambda i: (i, 0)),
        out_shape=jax.ShapeDtypeStruct(x2d.shape, F32),
        compiler_params=_cparams("arbitrary"),
        name="final_norm",
    )(x2d, gain.reshape(1, d))


def _mm_kernel(a_ref, b_ref, o_ref):
    o_ref[...] = jnp.dot(a_ref[...], b_ref[...], preferred_element_type=F32).astype(o_ref.dtype)


def _mm_res_kernel(a_ref, b_ref, x_ref, g_ref, o_ref):
    y = jnp.dot(a_ref[...], b_ref[...], preferred_element_type=F32)
    o_ref[...] = x_ref[...] + g_ref[0] * y


def _matmul(a, b, tm, tn, out_dtype=F32):
    m, k = a.shape
    n = b.shape[1]
    return pl.pallas_call(
        _mm_kernel,
        grid=(m // tm, n // tn),
        in_specs=[pl.BlockSpec((tm, k), lambda i, j: (i, 0)),
                  pl.BlockSpec((k, tn), lambda i, j: (0, j))],
        out_specs=pl.BlockSpec((tm, tn), lambda i, j: (i, j)),
        out_shape=jax.ShapeDtypeStruct((m, n), out_dtype),
        compiler_params=_cparams("arbitrary", "arbitrary"),
        name="matmul",
    )(a, b)


def _matmul_gated_residual(a, b, x2d, m3, row0, row_stride, chunk, bsz, t_len, tm, tn):
    m, k = a.shape
    n = b.shape[1]
    tm = min(tm, t_len)
    nt = t_len // tm
    return pl.pallas_call(
        _mm_res_kernel,
        grid=(bsz, nt, n // tn),
        in_specs=[pl.BlockSpec((tm, k), lambda bb, i, j: (bb * nt + i, 0)),
                  pl.BlockSpec((k, tn), lambda bb, i, j: (0, j)),
                  pl.BlockSpec((tm, tn), lambda bb, i, j: (bb * nt + i, j)),
                  _mod_spec(tn, row0, row_stride, chunk, n // tn, with_j=True)],
        out_specs=pl.BlockSpec((tm, tn), lambda bb, i, j: (bb * nt + i, j)),
        out_shape=jax.ShapeDtypeStruct((m, n), F32),
        compiler_params=_cparams("arbitrary", "arbitrary", "arbitrary"),
        name="matmul_gated_residual",
    )(a, b, x2d, m3)


def _dot_nt(a, b):
    return lax.dot_general(a, b, (((1,), (1,)), ((), ())), preferred_element_type=F32)


def _ctx_attn_kernel(q_ref, k_ref, v_ref, o_ref):
    q = q_ref[...]
    k = k_ref[...].astype(BF16)
    v = v_ref[...].astype(BF16)
    lane = lax.broadcasted_iota(jnp.int32, (1, LANES), 1)
    outs = []
    for hh in range(2):
        qh = jnp.where((lane >= hh * HEAD_DIM) & (lane < (hh + 1) * HEAD_DIM), q, 0.0).astype(BF16)
        s = _dot_nt(qh, k) * ATTN_SCALE
        e = jnp.exp(s - jnp.max(s, axis=-1, keepdims=True))
        p = e * (1.0 / jnp.sum(e, axis=-1, keepdims=True))
        outs.append(jnp.dot(p.astype(BF16), v, preferred_element_type=F32))
    o_ref[...] = jnp.where(lane < HEAD_DIM, outs[0], outs[1]).astype(o_ref.dtype)


def _context_attention(proj, bsz, t_len):
    pairs = A_W // LANES
    return pl.pallas_call(
        _ctx_attn_kernel,
        grid=(bsz, pairs),
        in_specs=[pl.BlockSpec((t_len, LANES), lambda b, p: (b, COL_Q // LANES + p)),
                  pl.BlockSpec((t_len, LANES), lambda b, p: (b, COL_K // LANES + p)),
                  pl.BlockSpec((t_len, LANES), lambda b, p: (b, COL_V // LANES + p))],
        out_specs=pl.BlockSpec((t_len, LANES), lambda b, p: (b, p)),
        out_shape=jax.ShapeDtypeStruct((bsz * t_len, A_W), BF16),
        compiler_params=_cparams("arbitrary", "arbitrary"),
        name="context_attention",
    )(proj, proj, proj)


def _nbr_attn_kernel(q_ref, k_ref, v_ref, ck_ref, cv_ref, tbl_ref, o_ref, *, rows):
    i = pl.program_id(1)
    kr = min(WIN_R, rows)
    rs = jnp.clip(i - kr // 2, 0, rows - kr)
    start = pl.multiple_of(rs * GRID_W, GRID_W)
    nloc = kr * GRID_W
    lane = lax.broadcasted_iota(jnp.int32, (1, LANES), 1)
    for p in range(A_W // LANES):
        cols = slice(p * LANES, (p + 1) * LANES)
        qp = q_ref[:, cols]
        kp = k_ref[pl.ds(start, nloc), cols].astype(BF16)
        vp = v_ref[pl.ds(start, nloc), cols].astype(BF16)
        ckp = ck_ref[0, 0, :, cols].astype(BF16)
        cvp = cv_ref[0, 0, :, cols].astype(BF16)
        outs = []
        for hh in range(2):
            qh = jnp.where((lane >= hh * HEAD_DIM) & (lane < (hh + 1) * HEAD_DIM), qp, 0.0).astype(BF16)
            tbl = tbl_ref[0, 2 * p + hh]
            s_loc = jnp.where(tbl > 0.5 * MASK_VALUE, _dot_nt(qh, kp) * ATTN_SCALE + tbl, MASK_VALUE)
            s_ctx = _dot_nt(qh, ckp) * ATTN_SCALE
            mx = jnp.maximum(jnp.max(s_loc, axis=-1, keepdims=True), jnp.max(s_ctx, axis=-1, keepdims=True))
            e_loc = jnp.exp(s_loc - mx)
            e_ctx = jnp.exp(s_ctx - mx)
            inv = 1.0 / (jnp.sum(e_loc, axis=-1, keepdims=True) + jnp.sum(e_ctx, axis=-1, keepdims=True))
            outs.append(jnp.dot((e_loc * inv).astype(BF16), vp, preferred_element_type=F32)
                        + jnp.dot((e_ctx * inv).astype(BF16), cvp, preferred_element_type=F32))
        o_ref[:, cols] = jnp.where(lane < HEAD_DIM, outs[0], outs[1]).astype(o_ref.dtype)


def _bias_tables(rpb_l, rows):
    kr = min(WIN_R, rows)
    offs = sorted({i - int(np.clip(i - kr // 2, 0, rows - kr)) for i in range(rows)})
    qcol = np.arange(GRID_W)
    kcol = np.arange(GRID_W)
    win_start = np.clip(qcol - WIN_C // 2, 0, GRID_W - WIN_C)
    valid = (kcol[None, :] >= win_start[:, None]) & (kcol[None, :] < win_start[:, None] + WIN_C)
    col_idx = np.clip(kcol[None, :] - qcol[:, None] + WIN_C - 1, 0, 2 * WIN_C - 2)
    tables = []
    for off in offs:
        row_idx = np.arange(kr) - off + WIN_R - 1
        b = rpb_l.astype(F32)[:, row_idx][:, :, col_idx]
        b = jnp.where(valid[None, None], b, MASK_VALUE).transpose(0, 2, 1, 3)
        tables.append(b.reshape(rpb_l.shape[0], GRID_W, kr * GRID_W))
    return jnp.stack(tables, 0), offs


def _neighbourhood_attention(proj, cache_k4, cache_v4, layer, rpb_l, bsz, t_len):
    rows = t_len // GRID_W
    kr = min(WIN_R, rows)
    tables, offs = _bias_tables(rpb_l, rows)
    assert offs == list(range(len(offs)))
    past = cache_k4.shape[2]

    def variant(i):
        return i - jnp.clip(i - kr // 2, 0, rows - kr)

    return pl.pallas_call(
        functools.partial(_nbr_attn_kernel, rows=rows),
        grid=(bsz, rows),
        in_specs=[pl.BlockSpec((GRID_W, A_W), lambda b, i: (b * rows + i, COL_Q // A_W)),
                  pl.BlockSpec((t_len, A_W), lambda b, i: (b, COL_K // A_W)),
                  pl.BlockSpec((t_len, A_W), lambda b, i: (b, COL_V // A_W)),
                  pl.BlockSpec((1, 1, past, A_W), lambda b, i: (b, layer, 0, 0)),
                  pl.BlockSpec((1, 1, past, A_W), lambda b, i: (b, layer, 0, 0)),
                  pl.BlockSpec((1, HA, GRID_W, kr * GRID_W), lambda b, i: (variant(i), 0, 0, 0))],
        out_specs=pl.BlockSpec((GRID_W, A_W), lambda b, i: (b * rows + i, 0)),
        out_shape=jax.ShapeDtypeStruct((bsz * t_len, A_W), BF16),
        compiler_params=_cparams("arbitrary", "arbitrary"),
        name="neighbourhood_attention",
    )(proj, proj, proj, cache_k4, cache_v4, tables)


CONV_CHUNK = 64
CONV_PAD = 16


def _fill_padded(pad_ref, values, t_len):
    width = pad_ref.shape[1]
    pad_ref[0:CONV_PAD, :] = jnp.zeros((CONV_PAD, width), F32)
    pad_ref[CONV_PAD + t_len:2 * CONV_PAD + t_len, :] = jnp.zeros((CONV_PAD, width), F32)
    pad_ref[CONV_PAD:CONV_PAD + t_len, :] = values


def _conv_taps(pad_ref, w_ref, t0, taps):
    left = (taps - 1) // 2
    window = pad_ref[pl.ds(t0, CONV_CHUNK + 2 * CONV_PAD), :]
    acc = jnp.zeros((CONV_CHUNK, pad_ref.shape[1]), F32)
    for kk in range(taps):
        off = CONV_PAD - left + kk
        acc = acc + window[off:off + CONV_CHUNK, :] * w_ref[kk:kk + 1, :]
    return acc


def _conv_module_kernel(a_ref, b_ref, w_ref, cb_ref, g_ref, lb_ref, o_ref, pad_ref):
    t_len = a_ref.shape[0]
    _fill_padded(pad_ref, a_ref[...] * _sigmoid(b_ref[...]), t_len)

    def chunk(c, carry):
        t0 = pl.multiple_of(c * CONV_CHUNK, CONV_CHUNK)
        u = _conv_taps(pad_ref, w_ref, t0, CONV_K) + cb_ref[...]
        mu = jnp.mean(u, axis=-1, keepdims=True)
        dlt = u - mu
        var = jnp.mean(dlt * dlt, axis=-1, keepdims=True)
        y = dlt * lax.rsqrt(var + LN_EPS) * g_ref[...] + lb_ref[...]
        o_ref[pl.ds(t0, CONV_CHUNK), :] = (y * _sigmoid(y)).astype(o_ref.dtype)
        return carry

    lax.fori_loop(0, t_len // CONV_CHUNK, chunk, 0)


def _conv_module(proj, conv_w, conv_b, ln_g, ln_b, bsz, t_len):
    row = lambda v: v.reshape(1, B_W)
    return pl.pallas_call(
        _conv_module_kernel,
        grid=(bsz,),
        in_specs=[pl.BlockSpec((t_len, B_W), lambda b: (b, COL_GA // B_W)),
                  pl.BlockSpec((t_len, B_W), lambda b: (b, COL_GB // B_W)),
                  pl.BlockSpec((CONV_K, B_W), lambda b: (0, 0)),
                  pl.BlockSpec((1, B_W), lambda b: (0, 0)),
                  pl.BlockSpec((1, B_W), lambda b: (0, 0)),
                  pl.BlockSpec((1, B_W), lambda b: (0, 0))],
        out_specs=pl.BlockSpec((t_len, B_W), lambda b: (b, 0)),
        out_shape=jax.ShapeDtypeStruct((bsz * t_len, B_W), BF16),
        scratch_shapes=[pltpu.VMEM((t_len + 2 * CONV_PAD, B_W), F32)],
        compiler_params=_cparams("arbitrary"),
        name="conv_module",
    )(proj, proj, conv_w, row(conv_b), row(ln_g), row(ln_b))


FFN_ROWS = 2048
FFN_COLS = 256


def _ffn_up_kernel(a_ref, wu_ref, wg_ref, cw_ref, cb_ref, o_ref, *, t_len):
    a = a_ref[...]
    u = jnp.dot(a, wu_ref[...], preferred_element_type=F32)
    g = jnp.dot(a, wg_ref[...], preferred_element_type=F32)
    rows = u.shape[0]
    pos = lax.broadcasted_iota(jnp.int32, (rows, 1), 0) % t_len
    prev = jnp.where(pos == 0, 0.0, pltpu.roll(u, 1, axis=0))
    nxt = jnp.where(pos == t_len - 1, 0.0, pltpu.roll(u, rows - 1, axis=0))
    y = prev * cw_ref[0:1, :] + u * cw_ref[1:2, :] + nxt * cw_ref[2:3, :] + cb_ref[...]
    o_ref[...] = (y * _sigmoid(y) * g).astype(o_ref.dtype)


def _ffn_up(h2, w_up, dw_w, dw_b, t_len):
    m, k = h2.shape
    assert FFN_ROWS % t_len == 0 and m % FFN_ROWS == 0
    nc = D_FF // FFN_COLS
    return pl.pallas_call(
        functools.partial(_ffn_up_kernel, t_len=t_len),
        grid=(m // FFN_ROWS, nc),
        in_specs=[pl.BlockSpec((FFN_ROWS, k), lambda i, j: (i, 0)),
                  pl.BlockSpec((k, FFN_COLS), lambda i, j: (0, j)),
                  pl.BlockSpec((k, FFN_COLS), lambda i, j: (0, nc + j)),
                  pl.BlockSpec((3, FFN_COLS), lambda i, j: (0, j)),
                  pl.BlockSpec((1, FFN_COLS), lambda i, j: (0, j))],
        out_specs=pl.BlockSpec((FFN_ROWS, FFN_COLS), lambda i, j: (i, j)),
        out_shape=jax.ShapeDtypeStruct((m, D_FF), BF16),
        compiler_params=_cparams("arbitrary", "arbitrary"),
        name="ffn_up",
    )(h2, w_up, w_up, dw_w, dw_b.reshape(1, D_FF))


def _segment_sum(x, ones_blockdiag):
    hi = x.astype(BF16)
    lo = (x - hi.astype(F32)).astype(BF16)
    return (jnp.dot(hi, ones_blockdiag, preferred_element_type=F32)
            + jnp.dot(lo, ones_blockdiag, preferred_element_type=F32))


def _rwkv_prep_kernel(r_ref, k_ref, v_ref, s_ref, kkw_ref, ka_ref, rk_ref, w0_ref, a0_ref, wl_ref, al_ref,
                      gl_ref, e_ref, z_o, wf_o, kdf_o, bf_o, wb_o, kdb_o, bb_o, bonus_o, g_o):
    r = r_ref[...]
    k = k_ref[...]
    e = e_ref[...]
    kk = k * kkw_ref[...]
    kk = kk * lax.rsqrt(_segment_sum(kk * kk, e) + L2_EPS)
    xs = s_ref[...]
    xw = jnp.tanh(xs[:, 0:LANES]).astype(BF16)
    xa = xs[:, LANES:2 * LANES].astype(BF16)
    xg = _sigmoid(xs[:, 2 * LANES:3 * LANES]).astype(BF16)
    w_all = jnp.exp(-DECAY_SCALE * _sigmoid(w0_ref[...] + jnp.dot(xw, wl_ref[...], preferred_element_type=F32)))
    a_all = _sigmoid(a0_ref[...] + jnp.dot(xa, al_ref[...], preferred_element_type=F32))
    g_o[...] = jnp.dot(xg, gl_ref[...], preferred_element_type=F32)
    z_o[...] = -kk
    ka = ka_ref[...]
    kd_sum = None
    for d, (w_o, kd_o, b_o) in enumerate(((wf_o, kdf_o, bf_o), (wb_o, kdb_o, bb_o))):
        a = a_all[:, d * C_W:(d + 1) * C_W]
        kd = k * (1.0 + (a - 1.0) * ka)
        w_o[...] = w_all[:, d * C_W:(d + 1) * C_W]
        kd_o[...] = kd
        b_o[...] = kk * a
        kd_sum = kd if kd_sum is None else kd_sum + kd
    bonus_o[...] = _segment_sum(r * kd_sum * rk_ref[...], e) * v_ref[...]


def _rwkv_prep(proj, lp, ones_blockdiag):
    m = proj.shape[0]
    tm = 256
    tok = lambda col: pl.BlockSpec((tm, C_W), lambda i, c=col // C_W: (i, c))
    const = lambda shape: pl.BlockSpec(shape, lambda i: (0, 0))
    out = jax.ShapeDtypeStruct((m, C_W), F32)
    return pl.pallas_call(
        _rwkv_prep_kernel,
        grid=(m // tm,),
        in_specs=[tok(COL_R), tok(COL_KC), tok(COL_VC),
                  pl.BlockSpec((tm, LORA_W), lambda i: (i, COL_LORA // LORA_W)),
                  const((1, C_W)), const((1, C_W)), const((1, C_W)),
                  const((1, 2 * C_W)), const((1, 2 * C_W)),
                  const((LANES, 2 * C_W)), const((LANES, 2 * C_W)), const((LANES, C_W)),
                  const((C_W, C_W))],
        out_specs=[pl.BlockSpec((tm, C_W), lambda i: (i, 0))] * 9,
        out_shape=[out] * 9,
        compiler_params=_cparams("arbitrary"),
        name="rwkv_prep",
    )(proj, proj, proj, proj, lp['k_k'], lp['k_a'], lp['r_k'], lp['w0'], lp['a0'],
      lp['w_lora'], lp['a_lora'], lp['g_lora'], ones_blockdiag)


V_SUB = HEAD_DIM // 2


def _scan_kernel(*refs, tb, mode):
    if mode == "mixed":
        fwd_in, bwd_in = refs[0:6], refs[6:12]
        s0_ref, o_ref, o_mir_ref, sf_ref, s_scr = refs[12:]
        lane = lax.broadcasted_iota(jnp.int32, (1, LANES), 1)
        fwd_lane = (lane % CHAIN_GROUP) < CHAIN_GROUP // 2
    else:
        fwd_in = bwd_in = refs[0:6]
        s0_ref, o_ref, sf_ref, s_scr = refs[6:]
        o_mir_ref = o_ref
    j = pl.program_id(1)

    def pick(idx, t, rows):
        if mode == "fwd":
            return fwd_in[idx][0, t, rows, :]
        if mode == "bwd":
            return bwd_in[idx][0, tb - 1 - t, rows, :]
        return jnp.where(fwd_lane, fwd_in[idx][0, t, rows, :], bwd_in[idx][0, tb - 1 - t, rows, :])

    def row(idx, t, p):
        return jnp.broadcast_to(pick(idx, t, pl.ds(p, 1)), (V_SUB, LANES))

    Z, R, W, K, B, V = range(6)

    @pl.when(j == 0)
    def _():
        s_scr[...] = s0_ref[0]

    u0 = jnp.zeros((V_SUB, LANES), F32)
    for p in range(HEAD_DIM):
        u0 = u0 + s_scr[p] * row(Z, 0, p)

    def step(t, u):
        t_next = jnp.minimum(t + 1, tb - 1)
        vt = pick(V, t, slice(None))
        o = jnp.zeros((V_SUB, LANES), F32)
        u_next = jnp.zeros((V_SUB, LANES), F32)
        for p in range(HEAD_DIM):
            s = s_scr[p] * row(W, t, p) + u * row(B, t, p) + vt * row(K, t, p)
            s_scr[p] = s
            o = o + s * row(R, t, p)
            u_next = u_next + s * row(Z, t_next, p)
        if mode != "bwd":
            o_ref[0, t] = o
        if mode != "fwd":
            o_mir_ref[0, tb - 1 - t] = o
        return u_next

    lax.fori_loop(0, tb, step, u0)

    @pl.when(j == pl.num_programs(1) - 1)
    def _():
        sf_ref[0] = s_scr[...]


def _wkv_scan_groups(arrays, s0, t_len, g0, ng, mode):
    tb = 32
    nj = t_len // tb
    nat = lambda g, j: (g0 + g, j, 0, 0)
    mir = lambda g, j: (g0 + g, nj - 1 - j, 0, 0)
    vec = lambda im: pl.BlockSpec((1, tb, HEAD_DIM, LANES), im)
    val = lambda im: pl.BlockSpec((1, tb, V_SUB, LANES), im)
    state = pl.BlockSpec((1, HEAD_DIM, V_SUB, LANES), lambda g, j: (g0 + g, 0, 0, 0))
    state_out = pl.BlockSpec((1, HEAD_DIM, V_SUB, LANES), lambda g, j: (g, 0, 0, 0))
    o_shape = jax.ShapeDtypeStruct((ng, t_len, V_SUB, LANES), F32)
    o_nat = pl.BlockSpec((1, tb, V_SUB, LANES), lambda g, j: (g, j, 0, 0))
    o_mir = pl.BlockSpec((1, tb, V_SUB, LANES), lambda g, j: (g, nj - 1 - j, 0, 0))
    specs = lambda im: [vec(im)] * 5 + [val(im)]
    if mode == "mixed":
        in_specs, operands = specs(nat) + specs(mir) + [state], list(arrays) * 2 + [s0]
        out_specs, out_shape = [o_nat, o_mir, state_out], [o_shape, o_shape]
    else:
        in_specs, operands = specs(nat if mode == "fwd" else mir) + [state], list(arrays) + [s0]
        out_specs, out_shape = [o_nat if mode == "fwd" else o_mir, state_out], [o_shape]
    out_shape = out_shape + [jax.ShapeDtypeStruct((ng, HEAD_DIM, V_SUB, LANES), F32)]
    res = pl.pallas_call(
        functools.partial(_scan_kernel, tb=tb, mode=mode),
        grid=(ng, nj),
        in_specs=in_specs,
        out_specs=out_specs,
        out_shape=out_shape,
        scratch_shapes=[pltpu.VMEM((HEAD_DIM, V_SUB, LANES), F32)],
        compiler_params=_cparams("arbitrary", "arbitrary"),
        name="wkv_scan_" + mode,
    )(*operands)
    return res[:-1], res[-1]


def _wkv_scan(arrays, s0, t_len):
    groups = s0.shape[0]
    half = groups // 2
    lane_fwd = (np.arange(LANES) % CHAIN_GROUP) < CHAIN_GROUP // 2
    outs, states = [], []
    if half:
        (o,), s = _wkv_scan_groups(arrays, s0, t_len, 0, half, "fwd")
        outs.append(o)
        states.append(s)
    if groups % 2:
        (o_f, o_b), s = _wkv_scan_groups(arrays, s0, t_len, half, 1, "mixed")
        outs.append(jnp.where(lane_fwd, o_f, o_b))
        states.append(s)
    if half:
        (o,), s = _wkv_scan_groups(arrays, s0, t_len, groups - half, half, "bwd")
        outs.append(o)
        states.append(s)
    return jnp.concatenate(outs, axis=0), jnp.concatenate(states, axis=0)


def _rwkv_post_kernel(of_ref, ob_ref, bonus_ref, g_ref, lg_ref, lb_ref, e_ref, o_ref):
    e = e_ref[...]
    o = of_ref[...] + ob_ref[...]
    mu = _segment_sum(o, e) * (1.0 / HEAD_DIM)
    dlt = o - mu
    var = _segment_sum(dlt * dlt, e) * (1.0 / HEAD_DIM)
    y = dlt * lax.rsqrt(var + GN_EPS) * lg_ref[...] + lb_ref[...] + bonus_ref[...]
    o_ref[...] = (y * g_ref[...]).astype(o_ref.dtype)


def _rwkv_post(o_f, o_b, bonus, g, ln_g, ln_b, ones_blockdiag):
    m = o_f.shape[0]
    tm = 256
    tok = pl.BlockSpec((tm, C_W), lambda i: (i, 0))
    const = lambda shape: pl.BlockSpec(shape, lambda i: (0, 0))
    return pl.pallas_call(
        _rwkv_post_kernel,
        grid=(m // tm,),
        in_specs=[tok, tok, tok, tok, const((1, C_W)), const((1, C_W)), const((C_W, C_W))],
        out_specs=tok,
        out_shape=jax.ShapeDtypeStruct((m, C_W), BF16),
        compiler_params=_cparams("arbitrary"),
        name="rwkv_post",
    )(o_f, o_b, bonus, g, ln_g, ln_b, ones_blockdiag)


def _key_vectors_to_chains(x_f, x_b, bsz, t_len):
    x = jnp.stack([x_f.reshape(bsz, t_len, HC, HEAD_DIM), x_b.reshape(bsz, t_len, HC, HEAD_DIM)], 0)
    groups = 2 * bsz * HC // CHAIN_GROUP
    x = x.transpose(2, 4, 0, 1, 3).reshape(t_len, HEAD_DIM, groups, 1, CHAIN_GROUP)
    x = jnp.broadcast_to(x, (t_len, HEAD_DIM, groups, 2, CHAIN_GROUP))
    return x.transpose(2, 0, 1, 3, 4).reshape(groups, t_len, HEAD_DIM, LANES)


def _values_to_chains(v, bsz, t_len):
    v = v.reshape(bsz, t_len, HC, 2, V_SUB)
    x = jnp.stack([v, v], 0)
    groups = 2 * bsz * HC // CHAIN_GROUP
    x = x.transpose(2, 5, 4, 0, 1, 3).reshape(t_len, V_SUB, 2, groups, CHAIN_GROUP)
    return x.transpose(3, 0, 1, 2, 4).reshape(groups, t_len, V_SUB, LANES)


def _values_from_chains(o, bsz, t_len):
    groups = o.shape[0]
    x = o.reshape(groups, t_len, V_SUB, 2, CHAIN_GROUP).transpose(1, 2, 3, 0, 4)
    x = x.reshape(t_len, V_SUB, 2, 2, bsz, HC).transpose(3, 4, 0, 5, 2, 1)
    x = x.reshape(2, bsz * t_len, C_W)
    return x[0], x[1]


def _states_to_chains(s_f, s_b, bsz):
    x = jnp.stack([s_f, s_b], 0).astype(F32).reshape(2, bsz, HC, 2, V_SUB, HEAD_DIM)
    groups = 2 * bsz * HC // CHAIN_GROUP
    x = x.transpose(5, 4, 3, 0, 1, 2).reshape(HEAD_DIM, V_SUB, 2, groups, CHAIN_GROUP)
    return x.transpose(3, 0, 1, 2, 4).reshape(groups, HEAD_DIM, V_SUB, LANES)


def _states_from_chains(s, bsz):
    groups = s.shape[0]
    x = s.reshape(groups, HEAD_DIM, V_SUB, 2, CHAIN_GROUP).transpose(1, 2, 3, 0, 4)
    x = x.reshape(HEAD_DIM, V_SUB, 2, 2, bsz, HC).transpose(3, 4, 5, 2, 1, 0)
    x = x.reshape(2, bsz, HC, HEAD_DIM, HEAD_DIM)
    return x[0], x[1]


def _rwkv7(proj, lp, ones_blockdiag, s0_f, s0_b, bsz, t_len):
    z, w_f, kd_f, b_f, w_b, kd_b, b_b, bonus, g = _rwkv_prep(proj, lp, ones_blockdiag)
    r = proj[:, COL_R:COL_R + C_W]
    v = proj[:, COL_VC:COL_VC + C_W]
    to_chains = functools.partial(_key_vectors_to_chains, bsz=bsz, t_len=t_len)
    arrays = (to_chains(z, z), to_chains(r, r), to_chains(w_f, w_b), to_chains(kd_f, kd_b),
              to_chains(b_f, b_b), _values_to_chains(v, bsz, t_len))
    o, s_fin = _wkv_scan(arrays, _states_to_chains(s0_f, s0_b, bsz), t_len)
    o_f, o_b = _values_from_chains(o, bsz, t_len)
    y_c = _rwkv_post(o_f, o_b, bonus, g, lp['ln_g'], lp['ln_b'], ones_blockdiag)
    s_f, s_b = _states_from_chains(s_fin, bsz)
    return y_c, s_f, s_b


def _trunk_layer(x2d, bsz, t_len, m3, row0, row_stride, attend, s0_f, s0_b, lp, ones_blockdiag):
    mod = dict(m3=m3, row0=row0, row_stride=row_stride, bsz=bsz, t_len=t_len)
    h = _norm_mod(x2d, lp['n1'], chunk_shift=0, chunk_scale=1, **mod)
    proj = _matmul(h, lp['w_in'], tm=min(512, x2d.shape[0]), tn=1024)
    y_a = attend(proj)
    y_b = _conv_module(proj, lp['conv_w'], lp['conv_b'], lp['conv_ln_g'], lp['conv_ln_b'], bsz, t_len)
    y_c, s_f, s_b = _rwkv7(proj, lp, ones_blockdiag, s0_f, s0_b, bsz, t_len)
    y = jnp.concatenate([y_a, y_b, y_c], axis=-1)
    x2d = _matmul_gated_residual(y, lp['w_out'], x2d, chunk=2, tm=512, tn=1024, **mod)
    h2 = _norm_mod(x2d, lp['n2'], chunk_shift=3, chunk_scale=4, **mod)
    act = _ffn_up(h2, lp['w_up'], lp['ffn_conv_w'], lp['ffn_conv_b'], t_len)
    x2d = _matmul_gated_residual(act, lp['w_down'], x2d, chunk=5, tm=512, tn=512, **mod)
    return x2d, proj, s_f, s_b


def _permute_w_in(w_in_l):
    sizes = (A_W, A_W, A_W, B_W, B_W, C_W, C_W, C_W, W_RANK, W_RANK, A_RANK, A_RANK, G_RANK)
    starts = np.concatenate([[0], np.cumsum(sizes)])
    part = lambda i: w_in_l[:, starts[i]:starts[i + 1]]
    q, k, v, ga, gb, r, kc, vc, xwf, xwb, xaf, xab, xg = (part(i) for i in range(len(sizes)))
    pad = jnp.zeros((w_in_l.shape[0], PROJ_COLS - int(starts[-1])), w_in_l.dtype)
    return jnp.concatenate([q, k, v, r, kc, vc, ga, gb, xwf, xwb, xaf, xab, xg, pad], axis=1).astype(BF16)


def _padded_lora(lora_fb, rank):
    z = jnp.zeros((rank, C_W), lora_fb.dtype)
    top = jnp.concatenate([lora_fb[0], z], axis=1)
    bot = jnp.concatenate([z, lora_fb[1]], axis=1)
    return jnp.concatenate([top, bot], axis=0).astype(BF16)


def kernel(x_prompt, x_sample, cache_k, cache_v, state_fwd, state_bwd, c, c_ctx, norm1_g, norm2_g, w_mod, b_mod, w_in, w_out, rpb, conv_w, conv_b, conv_ln_g, conv_ln_b, rwkv_w0, rwkv_w_lora, rwkv_a0, rwkv_a_lora, rwkv_g_lora, rwkv_k_k, rwkv_k_a, rwkv_r_k, rwkv_ln_g, rwkv_ln_b, w_up, ffn_conv_w, ffn_conv_b, w_down, final_g):
    bp, tp, d = x_prompt.shape
    bs, ts, _ = x_sample.shape
    depth = w_in.shape[0]
    past = cache_k.shape[2]
    assert bs + 1 <= MOD_ROWS

    cvecs = jnp.concatenate([c_ctx[None, :], c, jnp.zeros((MOD_ROWS - 1 - bs, d), F32)], axis=0)
    m3 = _modulation(cvecs, w_mod, b_mod).reshape(depth * MOD_ROWS, 1, N_MOD * d)

    head_of_lane = np.arange(C_W) // HEAD_DIM
    ones_blockdiag = jnp.asarray(head_of_lane[:, None] == head_of_lane[None, :], BF16)
    cache_k4 = cache_k.reshape(bs, depth, past, A_W)
    cache_v4 = cache_v.reshape(bs, depth, past, A_W)
    zero_state = jnp.zeros((bp, HC, HEAD_DIM, HEAD_DIM), F32)

    xp = x_prompt.reshape(bp * tp, d)
    xs = x_sample.reshape(bs * ts, d)
    ks, vs, sfs, sbs = [], [], [], []
    for l in range(depth):
        row = lambda v: v.reshape(1, -1)
        lp = {
            'n1': norm1_g[l], 'n2': norm2_g[l],
            'w_in': _permute_w_in(w_in[l]), 'w_out': w_out[l].astype(BF16),
            'conv_w': conv_w[l], 'conv_b': conv_b[l], 'conv_ln_g': conv_ln_g[l], 'conv_ln_b': conv_ln_b[l],
            'w0': rwkv_w0[l].reshape(1, 2 * C_W), 'a0': rwkv_a0[l].reshape(1, 2 * C_W),
            'w_lora': _padded_lora(rwkv_w_lora[l], W_RANK), 'a_lora': _padded_lora(rwkv_a_lora[l], A_RANK),
            'g_lora': rwkv_g_lora[l].astype(BF16),
            'k_k': row(rwkv_k_k[l]), 'k_a': row(rwkv_k_a[l]), 'r_k': row(rwkv_r_k[l]),
            'ln_g': row(rwkv_ln_g[l]), 'ln_b': row(rwkv_ln_b[l]),
            'w_up': w_up[l].astype(BF16), 'ffn_conv_w': ffn_conv_w[l], 'ffn_conv_b': ffn_conv_b[l],
            'w_down': w_down[l].astype(BF16),
        }
        ctx_attend = functools.partial(_context_attention, bsz=bp, t_len=tp)
        xp, proj_p, s_f, s_b = _trunk_layer(xp, bp, tp, m3, l * MOD_ROWS, 0, ctx_attend,
                                            zero_state, zero_state, lp, ones_blockdiag)
        ks.append(proj_p[:, COL_K:COL_K + A_W].reshape(bp, tp, HA, HEAD_DIM))
        vs.append(proj_p[:, COL_V:COL_V + A_W].reshape(bp, tp, HA, HEAD_DIM))
        sfs.append(s_f)
        sbs.append(s_b)
        nbr_attend = functools.partial(_neighbourhood_attention, cache_k4=cache_k4, cache_v4=cache_v4,
                                       layer=l, rpb_l=rpb[l], bsz=bs, t_len=ts)
        xs, _, _, _ = _trunk_layer(xs, bs, ts, m3, l * MOD_ROWS + 1, 1, nbr_attend,
                                   state_fwd[:, l], state_bwd[:, l], lp, ones_blockdiag)
    y_prompt = _final_norm(xp, final_g).reshape(bp, tp, d)
    y_sample = _final_norm(xs, final_g).reshape(bs, ts, d)
    return (y_prompt, y_sample, jnp.stack(ks, axis=1), jnp.stack(vs, axis=1),
            jnp.stack(sfs, axis=1), jnp.stack(sbs, axis=1))
```
